```python
import math
import jax, jax.numpy as jnp
from jax import lax
import numpy as np

D_MODEL = 1024
BATCH = 16
SEQ = 2048
DEPTH = 4
DEC_BATCH = 1
DEC_SEQ = 16384
PAST_LEN = 128

CHUNK = 128
SGU_WIDTH = 1024
SGU_GROUPS = 8
SGU_GROUP_DIM = SGU_WIDTH // SGU_GROUPS
SSD_INNER = 2 * D_MODEL
SSD_HEAD_DIM = 64
SSD_HEADS = SSD_INNER // SSD_HEAD_DIM
SSD_GROUPS = 4
SSD_HPG = SSD_HEADS // SSD_GROUPS
SSD_STATE = 128
SSD_CONV = 5
SSD_CONV_DIM = SSD_INNER + 2 * SSD_GROUPS * SSD_STATE
IN_WIDTH = 2 * SGU_WIDTH + SSD_INNER + SSD_CONV_DIM + 2 * SSD_HEADS + 2 * D_MODEL
PEER_HEADS = 8
PEER_NKEYS = 128
PEER_EXPERTS = PEER_NKEYS * PEER_NKEYS
PEER_KEY_DIM = 256
PEER_HALF = PEER_KEY_DIM // 2
PEER_TOPK = 16
PEER_BLOCK = 128
EPS = 1e-6

kernel_name = 'hybrid_sgu_ssd_peer_encoder'


def rmsnorm(x, gain):
    xf = x.astype(jnp.float32)
    y = xf * lax.rsqrt(jnp.mean(xf * xf, axis=-1, keepdims=True) + EPS)
    return (y * gain.astype(jnp.float32)).astype(x.dtype)


def modulate(h, shift, scale):
    return h * (1 + scale[:, None, :]) + shift[:, None, :]


def spatial_gating(u, v, sgu_gain, w_s, b_s):
    bsz, seq_len, _ = v.shape
    v = rmsnorm(v, sgu_gain)
    vc = v.reshape(bsz, seq_len // CHUNK, CHUNK, SGU_GROUPS, SGU_GROUP_DIM)
    mixed = jnp.einsum('gqp,bnpgc->bnqgc', w_s, vc) + b_s.T[:, :, None]
    return u * mixed.reshape(bsz, seq_len, SGU_WIDTH)


def centred_depthwise_conv(x, w, bias):
    y = lax.conv_general_dilated(
        x, w[:, None, :], window_strides=(1,),
        padding=[(SSD_CONV // 2, SSD_CONV // 2)],
        dimension_numbers=('NWC', 'WIO', 'NWC'),
        feature_group_count=x.shape[-1])
    return y + bias


def segsum_exp(a):
    cs = jnp.cumsum(a, axis=-1)
    diff = cs[..., :, None] - cs[..., None, :]
    t = a.shape[-1]
    mask = jnp.tril(jnp.ones((t, t), dtype=bool))
    return jnp.exp(jnp.where(mask, diff, -jnp.inf))


def ssd_scan(x, dt, a, b_in, c_in):
    bsz, seq_len = x.shape[:2]
    nc = seq_len // CHUNK
    xdt = (x * dt[..., None]).reshape(bsz, nc, CHUNK, SSD_GROUPS, SSD_HPG, SSD_HEAD_DIM)
    adt = (dt * a).reshape(bsz, nc, CHUNK, SSD_GROUPS, SSD_HPG).transpose(0, 3, 4, 1, 2)
    bc = b_in.reshape(bsz, nc, CHUNK, SSD_GROUPS, SSD_STATE)
    cc = c_in.reshape(bsz, nc, CHUNK, SSD_GROUPS, SSD_STATE)
    a_cs = jnp.cumsum(adt, axis=-1)
    cb = jnp.einsum('bclgn,bcsgn->bgcls', cc, bc)
    m = cb[:, :, None] * segsum_exp(adt)
    y_diag = jnp.einsum('bgrcls,bcsgrp->bclgrp', m, xdt)
    decay_to_end = jnp.exp(a_cs[..., -1:] - a_cs).transpose(0, 3, 4, 1, 2)
    states = jnp.einsum('bclgn,bclgrp->bcgrpn', bc, xdt * decay_to_end[..., None])
    chunk_decay = jnp.exp(a_cs[..., -1])

    def step(h, inp):
        dec, st = inp
        return h * dec[..., None, None] + st, h

    h0 = jnp.zeros((bsz, SSD_GROUPS, SSD_HPG, SSD_HEAD_DIM, SSD_STATE), x.dtype)
    _, prev = lax.scan(step, h0, (jnp.moveaxis(chunk_decay, -1, 0), jnp.moveaxis(states, 1, 0)))
    decay_from_start = jnp.exp(a_cs).transpose(0, 3, 4, 1, 2)
    y_off = jnp.einsum('bclgn,cbgrpn->bclgrp', cc, prev) * decay_from_start[..., None]
    return (y_diag + y_off).reshape(bsz, seq_len, SSD_HEADS, SSD_HEAD_DIM)


def ssd_mixer(z, xbc, dt_f_raw, dt_b_raw, conv_w, conv_b, dt_bias, a_log, d_skip, norm_gain):
    bsz, seq_len, _ = xbc.shape
    f32 = jnp.float32
    xbc = jax.nn.silu(centred_depthwise_conv(xbc, conv_w, conv_b))
    xs, bs, cs = jnp.split(xbc, [SSD_INNER, SSD_INNER + SSD_GROUPS * SSD_STATE], axis=-1)
    xh = xs.reshape(bsz, seq_len, SSD_HEADS, SSD_HEAD_DIM).astype(f32)
    bg = bs.reshape(bsz, seq_len, SSD_GROUPS, SSD_STATE).astype(f32)
    cg = cs.reshape(bsz, seq_len, SSD_GROUPS, SSD_STATE).astype(f32)
    a = -jnp.exp(a_log.astype(f32))
    dtb = dt_bias.astype(f32)
    dt_f = jax.nn.softplus(dt_f_raw.astype(f32) + dtb[0])
    dt_b = jax.nn.softplus(dt_b_raw.astype(f32) + dtb[1])
    y_f = ssd_scan(xh, dt_f, a[0], bg, cg)
    flip = lambda t: jnp.flip(t, axis=1)
    y_b = flip(ssd_scan(flip(xh), flip(dt_b), a[1], flip(bg), flip(cg)))
    y = y_f + y_b + d_skip.astype(f32)[:, None] * xh
    y = y.reshape(bsz, seq_len, SSD_INNER) * jax.nn.silu(z.astype(f32))
    yg = y.reshape(bsz, seq_len, SSD_GROUPS, SSD_INNER // SSD_GROUPS)
    yg = yg * lax.rsqrt(jnp.mean(yg * yg, axis=-1, keepdims=True) + EPS)
    y = yg.reshape(bsz, seq_len, SSD_INNER) * norm_gain.astype(f32)
    return y.astype(z.dtype)


def peer(h, w_query, sub_keys, expert_u, expert_v):
    bsz, seq_len, d = h.shape
    blocks = h.reshape(-1, PEER_BLOCK, d)
    keys = sub_keys.astype(jnp.float32)

    def block(t):
        q = jnp.dot(t, w_query).reshape(PEER_BLOCK, PEER_HEADS, 2, PEER_HALF).astype(jnp.float32)
        s = jnp.einsum('thin,ikn->thik', q, keys)
        top_s, top_i = lax.top_k(s, PEER_TOPK)
        cand_s = top_s[:, :, 0, :, None] + top_s[:, :, 1, None, :]
        cand_i = top_i[:, :, 0, :, None] * PEER_NKEYS + top_i[:, :, 1, None, :]
        cand_s = cand_s.reshape(PEER_BLOCK, PEER_HEADS, PEER_TOPK * PEER_TOPK)
        cand_i = cand_i.reshape(PEER_BLOCK, PEER_HEADS, PEER_TOPK * PEER_TOPK)
        best_s, best_j = lax.top_k(cand_s, PEER_TOPK)
        idx = jnp.take_along_axis(cand_i, best_j, axis=-1)
        g = jax.nn.softmax(best_s, axis=-1)
        u = jnp.take(expert_u, idx, axis=0)
        act = jax.nn.gelu(jnp.einsum('thkd,td->thk', u, t))
        v = jnp.take(expert_v, idx, axis=0)
        return jnp.einsum('thk,thkd->td', (g * act).astype(t.dtype), v)

    return lax.map(block, blocks).reshape(bsz, seq_len, d)


def trunk(x, c, w_mod, b_mod, norm1_gain, norm2_gain, w_in, sgu_gain, w_spatial, b_spatial,
          conv_w, conv_b, dt_bias, a_log, d_skip, ssd_gain, w_proj_a, w_proj_b, w_out,
          w_query, sub_keys, expert_u, expert_v, final_gain):
    sizes = (SGU_WIDTH, SGU_WIDTH, SSD_INNER, SSD_CONV_DIM, SSD_HEADS, SSD_HEADS, D_MODEL, D_MODEL)
    offsets = [int(o) for o in np.cumsum(sizes)[:-1]]
    c_act = jax.nn.silu(c)
    for l in range(DEPTH):
        mod = jnp.dot(c_act, w_mod[l]) + b_mod[l]
        sh1, sc1, g1, sh2, sc2, g2 = jnp.split(mod, 6, axis=-1)
        h = modulate(rmsnorm(x, norm1_gain[l]), sh1, sc1)
        proj = jnp.dot(h, w_in[l])
        u, v, z, xbc, dtf, dtb, ga, gb = jnp.split(proj, offsets, axis=-1)
        ya = spatial_gating(jax.nn.gelu(u), jax.nn.gelu(v), sgu_gain[l], w_spatial[l], b_spatial[l])
        yb = ssd_mixer(z, xbc, dtf, dtb, conv_w[l], conv_b[l], dt_bias[l], a_log[l], d_skip[l], ssd_gain[l])
        merged = (jax.nn.sigmoid(ga) * jnp.dot(ya, w_proj_a[l])
                  + jax.nn.sigmoid(gb) * jnp.dot(yb, w_proj_b[l]))
        x = x + g2.dtype.type(1) * 0 + g1[:, None, :] * jnp.dot(merged, w_out[l]) if False else x + g1[:, None, :] * jnp.dot(merged, w_out[l])
        h = modulate(rmsnorm(x, norm2_gain[l]), sh2, sc2)
        x = x + g2[:, None, :] * peer(h, w_query[l], sub_keys[l], expert_u[l], expert_v[l])
    return rmsnorm(x, final_gain)


def setup_inputs(seed: int = 0) -> dict:
    key = jax.random.key(seed)
    ks = jax.random.split(key, 32)
    f32 = jnp.float32

    def nrm(k, shape, scale):
        return jax.random.normal(k, shape, f32) * scale

    def gain(k, shape):
        return 1.0 + 0.02 * jax.random.normal(k, shape, f32)

    dt_init = jnp.exp(jax.random.uniform(ks[14], (DEPTH, 2, SSD_HEADS), f32,
                                         minval=math.log(1e-3), maxval=math.log(1e-1)))
    return {
        'x_prompt': nrm(ks[0], (BATCH, SEQ, D_MODEL), 1.0),
        'x_sample': nrm(ks[1], (DEC_BATCH, DEC_SEQ, D_MODEL), 1.0),
        'c_prompt': nrm(ks[2], (BATCH, D_MODEL), 1.0),
        'c_sample': nrm(ks[3], (DEC_BATCH, D_MODEL), 1.0),
        'w_mod': nrm(ks[4], (DEPTH, D_MODEL, 6 * D_MODEL), 0.5 * D_MODEL ** -0.5),
        'b_mod': nrm(ks[5], (DEPTH, 6 * D_MODEL), 0.02),
        'norm1_gain': gain(ks[6], (DEPTH, D_MODEL)),
        'norm2_gain': gain(ks[7], (DEPTH, D_MODEL)),
        'w_in': nrm(ks[8], (DEPTH, D_MODEL, IN_WIDTH), D_MODEL ** -0.5),
        'sgu_gain': gain(ks[9], (DEPTH, SGU_WIDTH)),
        'w_spatial': nrm(ks[10], (DEPTH, SGU_GROUPS, CHUNK, CHUNK), CHUNK ** -0.5),
        'b_spatial': gain(ks[11], (DEPTH, SGU_GROUPS, CHUNK)),
        'conv_w': nrm(ks[12], (DEPTH, SSD_CONV, SSD_CONV_DIM), SSD_CONV ** -0.5),
        'conv_b': nrm(ks[13], (DEPTH, SSD_CONV_DIM), 0.02),
        'dt_bias': dt_init + jnp.log(-jnp.expm1(-dt_init)),
        'a_log': jnp.log(jax.random.uniform(ks[15], (DEPTH, 2, SSD_HEADS), f32, minval=1.0, maxval=16.0)),
        'd_skip': gain(ks[16], (DEPTH, SSD_HEADS)),
        'ssd_gain': gain(ks[17], (DEPTH, SSD_INNER)),
        'w_proj_a': nrm(ks[18], (DEPTH, SGU_WIDTH, D_MODEL), SGU_WIDTH ** -0.5),
        'w_proj_b': nrm(ks[19], (DEPTH, SSD_INNER, D_MODEL), SSD_INNER ** -0.5),
        'w_out': nrm(ks[20], (DEPTH, D_MODEL, D_MODEL), D_MODEL ** -0.5),
        'w_query': nrm(ks[21], (DEPTH, D_MODEL, PEER_HEADS * PEER_KEY_DIM), D_MODEL ** -0.5),
        'sub_keys': nrm(ks[22], (DEPTH, 2, PEER_NKEYS, PEER_HALF), PEER_HALF ** -0.5),
        'expert_u': nrm(ks[23], (DEPTH, PEER_EXPERTS, D_MODEL), D_MODEL ** -0.5),
        'expert_v': nrm(ks[24], (DEPTH, PEER_EXPERTS, D_MODEL), PEER_HEADS ** -0.5),
        'final_gain': gain(ks[25], (D_MODEL,)),
    }


def reference(x_prompt, x_sample, c_prompt, c_sample, w_mod, b_mod, norm1_gain, norm2_gain,
              w_in, sgu_gain, w_spatial, b_spatial, conv_w, conv_b, dt_bias, a_log, d_skip,
              ssd_gain, w_proj_a, w_proj_b, w_out, w_query, sub_keys, expert_u, expert_v,
              final_gain):
    params = (w_mod, b_mod, norm1_gain, norm2_gain, w_in, sgu_gain, w_spatial, b_spatial,
              conv_w, conv_b, dt_bias, a_log, d_skip, ssd_gain, w_proj_a, w_proj_b, w_out,
              w_query, sub_keys, expert_u, expert_v, final_gain)
    y_prompt = trunk(x_prompt, c_prompt, *params)
    y_sample = trunk(x_sample, c_sample, *params)
    return (y_prompt, y_sample)
```

```python
import functools

import jax
import jax.numpy as jnp
from jax import lax
from jax.experimental import pallas as pl
from jax.experimental.pallas import tpu as pltpu

F32 = jnp.float32
BF16 = jnp.bfloat16

D_MODEL = 1024
DEPTH = 4
CHUNK = 128
SGU_WIDTH = 1024
SGU_GROUPS = 8
SSD_INNER = 2048
SSD_HEAD_DIM = 64
SSD_HEADS = 32
SSD_GROUPS = 4
SSD_STATE = 128
SSD_CONV = 5
SSD_CONV_DIM = 3072
PEER_HEADS = 8
PEER_NKEYS = 128
PEER_EXPERTS = PEER_NKEYS * PEER_NKEYS
PEER_HALF = 128
PEER_TOPK = 16
EPS = 1e-6

LANES = 128
MIB = 1024 * 1024
NEG_BIG = -3.0e38

OFF_UV, OFF_Z, OFF_XBC, OFF_DT, OFF_GATE, OFF_END = 0, 2048, 4096, 7168, 7232, 9280


def _params(semantics, vmem_mib=48):
    return pltpu.CompilerParams(dimension_semantics=semantics, vmem_limit_bytes=vmem_mib * MIB)


def _sigmoid(x):
    return 1.0 / (1.0 + jnp.exp(-x))


def _silu(x):
    return x * _sigmoid(x)


def _gelu(x):
    return 0.5 * x * (1.0 + jnp.tanh(0.7978845608028654 * (x + 0.044715 * (x * x * x))))


def _softplus(x):
    return jnp.maximum(x, 0.0) + jnp.log(1.0 + jnp.exp(-jnp.abs(x)))


def _identity(x):
    return x


def _mod_kernel(c_ref, w_ref, b_ref, o_ref):
    c = c_ref[...]
    o_ref[0] = jnp.dot(_silu(c), w_ref[0], preferred_element_type=F32,
                       precision=lax.Precision.HIGHEST) + b_ref[0]


def _modulation(c, w_mod, b_mod):
    bsz = c.shape[0]
    bp = -(-bsz // 8) * 8
    cp = jnp.pad(c, ((0, bp - bsz), (0, 0)))
    out = pl.pallas_call(
        _mod_kernel,
        out_shape=jax.ShapeDtypeStruct((DEPTH, bp, 6 * D_MODEL), F32),
        grid=(DEPTH, 6),
        in_specs=[pl.BlockSpec((bp, D_MODEL), lambda l, j: (0, 0)),
                  pl.BlockSpec((1, D_MODEL, D_MODEL), lambda l, j: (l, 0, j)),
                  pl.BlockSpec((1, 1, D_MODEL), lambda l, j: (l, 0, j))],
        out_specs=pl.BlockSpec((1, bp, D_MODEL), lambda l, j: (l, 0, j)),
        compiler_params=_params(("parallel", "parallel")),
        name="modulation",
    )(cp, w_mod, b_mod.reshape(DEPTH, 1, 6 * D_MODEL))
    return out[:, :bsz].reshape(DEPTH, bsz, 6, 1, D_MODEL)


def _prenorm_kernel(x_ref, gain_ref, sh_ref, sc_ref, o_ref):
    x = x_ref[0]
    y = x * lax.rsqrt(jnp.mean(x * x, axis=-1, keepdims=True) + EPS) * gain_ref[...]
    o_ref[0] = (y * (1.0 + sc_ref[...]) + sh_ref[...]).astype(o_ref.dtype)


def _prenorm(x, gain, mod, shift_idx, scale_idx, tl=512):
    bsz, seq, _ = x.shape
    tl = min(tl, seq)
    return pl.pallas_call(
        _prenorm_kernel,
        out_shape=jax.ShapeDtypeStruct((bsz, seq, D_MODEL), BF16),
        grid=(bsz, seq // tl),
        in_specs=[pl.BlockSpec((1, tl, D_MODEL), lambda b, i: (b, i, 0)),
                  pl.BlockSpec((1, D_MODEL), lambda b, i: (0, 0)),
                  pl.BlockSpec((None, None, 1, D_MODEL), lambda b, i: (b, shift_idx, 0, 0)),
                  pl.BlockSpec((None, None, 1, D_MODEL), lambda b, i: (b, scale_idx, 0, 0))],
        out_specs=pl.BlockSpec((1, tl, D_MODEL), lambda b, i: (b, i, 0)),
        compiler_params=_params(("parallel", "parallel")),
        name="prenorm",
    )(x, gain.reshape(1, D_MODEL), mod, mod)


def _matmul_kernel(a_ref, w_ref, o_ref, *, act):
    acc = jnp.dot(a_ref[...], w_ref[...], preferred_element_type=F32)
    o_ref[...] = act(acc).astype(o_ref.dtype)


def _matmul_act(a, w, act, out_dtype, tm, name):
    m, k = a.shape
    n = w.shape[1]
    tm = min(tm, m)
    return pl.pallas_call(
        functools.partial(_matmul_kernel, act=act),
        out_shape=jax.ShapeDtypeStruct((m, n), out_dtype),
        grid=(m // tm,),
        in_specs=[pl.BlockSpec((tm, k), lambda i: (i, 0)),
                  pl.BlockSpec((k, n), lambda i: (0, 0))],
        out_specs=pl.BlockSpec((tm, n), lambda i: (i, 0)),
        compiler_params=_params(("parallel",)),
        name=name,
    )(a, w)


def _sgu_kernel(uv_ref, ga_ref, gain_ref, ws_ref, bs_ref, wa_ref, o_ref, ya_ref):
    rows = uv_ref.shape[0]
    v = uv_ref[:, SGU_WIDTH:].astype(F32)
    vn = (v * lax.rsqrt(jnp.mean(v * v, axis=-1, keepdims=True) + EPS) * gain_ref[...]).astype(BF16)
    gdim = SGU_WIDTH // SGU_GROUPS
    for n in range(rows // CHUNK):
        r0 = n * CHUNK
        for g in range(SGU_GROUPS):
            c0 = g * gdim
            mixed = jnp.dot(ws_ref[g], vn[r0:r0 + CHUNK, c0:c0 + gdim],
                            preferred_element_type=F32) + bs_ref[g]
            u = uv_ref[r0:r0 + CHUNK, c0:c0 + gdim].astype(F32)
            ya_ref[r0:r0 + CHUNK, c0:c0 + gdim] = (u * mixed).astype(BF16)
    pa = jnp.dot(ya_ref[...], wa_ref[...], preferred_element_type=F32)
    o_ref[...] = (ga_ref[...].astype(F32) * pa).astype(o_ref.dtype)


def _sgu(uv, gates, sgu_gain, ws, bs_full, wa, ts=256):
    t = uv.shape[0]
    return pl.pallas_call(
        _sgu_kernel,
        out_shape=jax.ShapeDtypeStruct((t, D_MODEL), BF16),
        grid=(t // ts,),
        in_specs=[pl.BlockSpec((ts, 2 * SGU_WIDTH), lambda i: (i, 0)),
                  pl.BlockSpec((ts, D_MODEL), lambda i: (i, 0)),
                  pl.BlockSpec((1, SGU_WIDTH), lambda i: (0, 0)),
                  pl.BlockSpec((SGU_GROUPS, CHUNK, CHUNK), lambda i: (0, 0, 0)),
                  pl.BlockSpec((SGU_GROUPS, CHUNK, CHUNK), lambda i: (0, 0, 0)),
                  pl.BlockSpec((SGU_WIDTH, D_MODEL), lambda i: (0, 0))],
        out_specs=pl.BlockSpec((ts, D_MODEL), lambda i: (i, 0)),
        scratch_shapes=[pltpu.VMEM((ts, SGU_WIDTH), BF16)],
        compiler_params=_params(("parallel",)),
        name="sgu_proj_a",
    )(uv, gates, sgu_gain.reshape(1, SGU_WIDTH), ws, bs_full, wa)


def _conv_kernel(x_ref, prev_ref, next_ref, w_ref, b_ref, o_ref, ext_ref):
    i = pl.program_id(1)
    last = pl.num_programs(1) - 1
    tl = x_ref.shape[1]
    halo = prev_ref.shape[1]
    ext_ref[0:halo, :] = jnp.where(i > 0, prev_ref[0], 0.0)
    ext_ref[halo:halo + tl, :] = x_ref[0]
    ext_ref[halo + tl:, :] = jnp.where(i < last, next_ref[0], 0.0)
    acc = jnp.zeros((tl, x_ref.shape[2]), F32) + b_ref[...]
    for k in range(SSD_CONV):
        start = halo - SSD_CONV // 2 + k
        acc = acc + w_ref[k:k + 1, :] * ext_ref[start:start + tl, :]
    o_ref[0] = _silu(acc).astype(o_ref.dtype)


def _conv_silu(xbc, conv_w, conv_b, tl=512, tc=512):
    bsz, seq, ch = xbc.shape
    tl = min(tl, seq)
    halo = 8
    nblk = tl // halo
    last_blk = seq // halo - 1
    return pl.pallas_call(
        _conv_kernel,
        out_shape=jax.ShapeDtypeStruct((bsz, seq, ch), BF16),
        grid=(bsz, seq // tl, ch // tc),
        in_specs=[pl.BlockSpec((1, tl, tc), lambda b, i, c: (b, i, c)),
                  pl.BlockSpec((1, halo, tc), lambda b, i, c: (b, jnp.maximum(i * nblk - 1, 0), c)),
                  pl.BlockSpec((1, halo, tc), lambda b, i, c: (b, jnp.minimum((i + 1) * nblk, last_blk), c)),
                  pl.BlockSpec((SSD_CONV, tc), lambda b, i, c: (0, c)),
                  pl.BlockSpec((1, tc), lambda b, i, c: (0, c))],
        out_specs=pl.BlockSpec((1, tl, tc), lambda b, i, c: (b, i, c)),
        scratch_shapes=[pltpu.VMEM((tl + 2 * halo, tc), F32)],
        compiler_params=_params(("parallel", "parallel", "parallel")),
        name="conv_silu",
    )(xbc, xbc, xbc, conv_w, conv_b.reshape(1, ch))


def _ssd_kernel(xbc_ref, dt_ref, bias_ref, alog_ref, o_ref, state_ref, *, reverse):
    c = pl.program_id(1)

    @pl.when(c == 0)
    def _():
        state_ref[...] = jnp.zeros_like(state_ref)

    col0 = SSD_HEADS if reverse else 0
    row = lax.broadcasted_iota(jnp.int32, (CHUNK, CHUNK), 0)
    lane = lax.broadcasted_iota(jnp.int32, (CHUNK, CHUNK), 1)
    tri = (row <= lane) if reverse else (row >= lane)
    lo_half = lane < SSD_HEAD_DIM

    dt = _softplus(dt_ref[0] + bias_ref[...])
    adt = dt * (-jnp.exp(alog_ref[...]))
    cs = adt
    shift = 1
    while shift < CHUNK:
        if reverse:
            moved = pltpu.roll(cs, CHUNK - shift, axis=0)
            cs = cs + jnp.where(row < CHUNK - shift, moved, 0.0)
        else:
            moved = pltpu.roll(cs, shift, axis=0)
            cs = cs + jnp.where(row >= shift, moved, 0.0)
        shift *= 2
    cs_t = cs.T
    dt_t = dt.T
    end = 0 if reverse else CHUNK - 1
    cs_end = jnp.broadcast_to(cs_t[:, end:end + 1], (CHUNK, CHUNK))
    w_all = dt_t * jnp.exp(cs_end - cs_t)
    dec_all = jnp.exp(cs_end)

    hpg = SSD_HEADS // SSD_GROUPS
    for g in range(SSD_GROUPS):
        b_off = SSD_INNER + g * SSD_STATE
        c_off = SSD_INNER + SSD_GROUPS * SSD_STATE + g * SSD_STATE
        bg = xbc_ref[0, :, b_off:b_off + SSD_STATE]
        cg = xbc_ref[0, :, c_off:c_off + SSD_STATE]
        cb = lax.dot_general(cg, bg, (((1,), (1,)), ((), ())), preferred_element_type=F32)
        cg32 = cg.astype(F32)
        bg_t = bg.astype(F32).T
        for j in range(hpg // 2):
            pair = g * (hpg // 2) + j
            lhs_parts, lhs2_parts = [], []
            for k in range(2):
                col = col0 + 2 * pair + k
                lmat = jnp.broadcast_to(cs[:, col:col + 1], (CHUNK, CHUNK))
                decay = jnp.where(tri, jnp.exp(lmat - cs_t[col:col + 1, :]), 0.0)
                lhs_parts.append((cb * decay * dt_t[col:col + 1, :]).astype(BF16))
                lhs_parts.append((cg32 * jnp.exp(lmat)).astype(BF16))
                lhs2_parts.append((bg_t * w_all[col:col + 1, :]).astype(BF16))
            xs = xbc_ref[0, :, pair * LANES:(pair + 1) * LANES]
            zero = jnp.zeros_like(xs)
            x0 = jnp.where(lo_half, xs, zero)
            x1 = jnp.where(lo_half, zero, xs)
            st = state_ref[pair]
            st16 = st.astype(BF16)
            s0 = jnp.where(lo_half, st16, zero)
            s1 = jnp.where(lo_half, zero, st16)
            lhs = jnp.concatenate(lhs_parts, axis=1)
            rhs = jnp.concatenate([x0, s0, x1, s1], axis=0)
            o_ref[0, :, pair * LANES:(pair + 1) * LANES] = jnp.dot(
                lhs, rhs, preferred_element_type=F32).astype(o_ref.dtype)
            lhs2 = jnp.concatenate(lhs2_parts, axis=1)
            rhs2 = jnp.concatenate([x0, x1], axis=0)
            col_a = col0 + 2 * pair
            dec = jnp.where(lo_half, dec_all[col_a:col_a + 1, :], dec_all[col_a + 1:col_a + 2, :])
            state_ref[pair] = st * dec + jnp.dot(lhs2, rhs2, preferred_element_type=F32)


def _ssd_scan(xbc_act, dt_raw, dt_bias_row, alog_row, reverse):
    bsz, seq, _ = xbc_act.shape
    nc = seq // CHUNK
    if reverse:
        cmap = lambda b, c: (b, nc - 1 - c, 0)
    else:
        cmap = lambda b, c: (b, c, 0)
    return pl.pallas_call(
        functools.partial(_ssd_kernel, reverse=reverse),
        out_shape=jax.ShapeDtypeStruct((bsz, seq, SSD_INNER), F32),
        grid=(bsz, nc),
        in_specs=[pl.BlockSpec((1, CHUNK, SSD_CONV_DIM), cmap),
                  pl.BlockSpec((1, CHUNK, LANES), cmap),
                  pl.BlockSpec((1, LANES), lambda b, c: (0, 0)),
                  pl.BlockSpec((1, LANES), lambda b, c: (0, 0))],
        out_specs=pl.BlockSpec((1, CHUNK, SSD_INNER), cmap),
        scratch_shapes=[pltpu.VMEM((SSD_HEADS // 2, SSD_STATE, LANES), F32)],
        compiler_params=_params(("parallel", "arbitrary")),
        name="ssd_bwd" if reverse else "ssd_fwd",
    )(xbc_act, dt_raw, dt_bias_row, alog_row)


def _tail_kernel(yf_ref, yb_ref, xs_ref, z_ref, gb_ref, pa_ref, x_ref, g1_ref, dskip_ref, gain_ref,
                 wb_ref, wo_ref, n2_ref, sh2_ref, sc2_ref, xo_ref, h2_ref):
    y = yf_ref[...] + yb_ref[...] + dskip_ref[...] * xs_ref[...].astype(F32)
    y = y * z_ref[...].astype(F32)
    gw = SSD_INNER // SSD_GROUPS
    parts = []
    for g in range(SSD_GROUPS):
        yg = y[:, g * gw:(g + 1) * gw]
        parts.append(yg * lax.rsqrt(jnp.mean(yg * yg, axis=-1, keepdims=True) + EPS))
    yn = (jnp.concatenate(parts, axis=1) * gain_ref[...]).astype(BF16)
    pb = jnp.dot(yn, wb_ref[...], preferred_element_type=F32)
    merged = pa_ref[...].astype(F32) + gb_ref[...].astype(F32) * pb
    out = jnp.dot(merged.astype(BF16), wo_ref[...], preferred_element_type=F32)
    x = x_ref[...] + g1_ref[...] * out
    xo_ref[...] = x
    h = x * lax.rsqrt(jnp.mean(x * x, axis=-1, keepdims=True) + EPS) * n2_ref[...]
    h2_ref[...] = (h * (1.0 + sc2_ref[...]) + sh2_ref[...]).astype(h2_ref.dtype)


def _tail(yf, yb, xbc_act, z_act, gates, pa, x, mod, dskip_row, ssd_gain, wb, wo, norm2_gain, seq, tm=256):
    t = x.shape[0]
    tm = min(tm, seq)
    per_row = seq // tm
    row = lambda i: (i, 0)
    const = lambda i: (0, 0)
    modspec = lambda k: pl.BlockSpec((None, None, 1, D_MODEL), lambda i: (i // per_row, k, 0, 0))
    return pl.pallas_call(
        _tail_kernel,
        out_shape=(jax.ShapeDtypeStruct((t, D_MODEL), F32), jax.ShapeDtypeStruct((t, D_MODEL), BF16)),
        grid=(t // tm,),
        in_specs=[pl.BlockSpec((tm, SSD_INNER), row),
                  pl.BlockSpec((tm, SSD_INNER), row),
                  pl.BlockSpec((tm, SSD_INNER), row),
                  pl.BlockSpec((tm, SSD_INNER), row),
                  pl.BlockSpec((tm, D_MODEL), lambda i: (i, 1)),
                  pl.BlockSpec((tm, D_MODEL), row),
                  pl.BlockSpec((tm, D_MODEL), row),
                  modspec(2),
                  pl.BlockSpec((1, SSD_INNER), const),
                  pl.BlockSpec((1, SSD_INNER), const),
                  pl.BlockSpec((SSD_INNER, D_MODEL), const),
                  pl.BlockSpec((D_MODEL, D_MODEL), const),
                  pl.BlockSpec((1, D_MODEL), const),
                  modspec(3),
                  modspec(4)],
        out_specs=(pl.BlockSpec((tm, D_MODEL), row), pl.BlockSpec((tm, D_MODEL), row)),
        compiler_params=_params(("parallel",)),
        name="mix_tail",
    )(yf, yb, xbc_act, z_act, gates, pa, x, mod, dskip_row, ssd_gain.reshape(1, SSD_INNER), wb, wo,
      norm2_gain.reshape(1, D_MODEL), mod, mod)


N_RANK = PEER_TOPK + 1
CAND_PAIRS = [(a, b) for a in range(N_RANK) for b in range(N_RANK) if (a + 1) * (b + 1) <= N_RANK]


def _route_kernel(h_ref, wq_ref, keys_ref, ns0_ref, a_ref, s1p_ref, b_ref, top_ref, s_ref):
    tt = h_ref.shape[0]
    q_t = lax.dot_general(wq_ref[...], h_ref[...], (((1,), (1,)), ((), ())),
                          preferred_element_type=F32)
    for hd in range(PEER_HEADS):
        for half in range(2):
            r0 = (hd * 2 + half) * PEER_HALF
            q = q_t[r0:r0 + PEER_HALF, :].astype(BF16)
            s = jnp.dot(keys_ref[half], q, preferred_element_type=F32)
            s_ref[half, hd] = s
            work = s
            for r in range(N_RANK):
                m = jnp.max(work, axis=0, keepdims=True)
                top_ref[half, r, hd:hd + 1, :] = m
                work = jnp.where(work == m, NEG_BIG, work)
    cands = [top_ref[0, a] + top_ref[1, b] for a, b in CAND_PAIRS]
    best = cands[0]
    work = list(cands)
    kth = []
    for r in range(N_RANK):
        m = work[0]
        for x in work[1:]:
            m = jnp.maximum(m, x)
        kth.append(m)
        if r + 1 < N_RANK:
            work = [jnp.where(x == m, NEG_BIG, x) for x in work]
    tau = 0.5 * (kth[PEER_TOPK - 1] + kth[PEER_TOPK])
    z = jnp.zeros_like(best)
    for x in cands:
        z = z + jnp.where(x >= tau, jnp.exp(x - best), 0.0)
    inv_z = 1.0 / z
    for hd in range(PEER_HEADS):
        s0 = s_ref[0, hd]
        s1 = s_ref[1, hd]
        ns0_ref[hd] = -s0
        a_ref[hd] = jnp.exp(s0 - top_ref[0, 0, hd:hd + 1, :])
        s1p_ref[hd] = s1 - tau[hd:hd + 1, :]
        b_ref[hd] = jnp.exp(s1 - top_ref[1, 0, hd:hd + 1, :]) * inv_z[hd:hd + 1, :]


def _route(h2, wq_t, keys, tt=256):
    t = h2.shape[0]
    shp = jax.ShapeDtypeStruct((PEER_HEADS, PEER_NKEYS, t), F32)
    ospec = pl.BlockSpec((PEER_HEADS, PEER_NKEYS, tt), lambda i: (0, 0, i))
    return pl.pallas_call(
        _route_kernel,
        out_shape=(shp, shp, shp, shp),
        grid=(t // tt,),
        in_specs=[pl.BlockSpec((tt, D_MODEL), lambda i: (i, 0)),
                  pl.BlockSpec((2 * PEER_HEADS * PEER_HALF, D_MODEL), lambda i: (0, 0)),
                  pl.BlockSpec((2, PEER_NKEYS, PEER_HALF), lambda i: (0, 0, 0))],
        out_specs=(ospec, ospec, ospec, ospec),
        scratch_shapes=[pltpu.VMEM((2, N_RANK, PEER_HEADS, tt), F32),
                        pltpu.VMEM((2, PEER_HEADS, PEER_NKEYS, tt), F32)],
        compiler_params=_params(("parallel",)),
        name="peer_route",
    )(h2, wq_t, keys)


E_BLK = 1024
I_BLK = E_BLK // PEER_NKEYS
J_SUB = 32


def _peer_kernel(h_ref, u_ref, vt_ref, ns0_ref, a_ref, s1p_ref, b_ref, x_ref, g2_ref, fin_ref, o_ref,
                 g_ref, p_ref, acc_ref, *, final_norm):
    e = pl.program_id(1)
    tt = h_ref.shape[0]

    @pl.when(e == 0)
    def _():
        acc_ref[...] = jnp.zeros_like(acc_ref)

    act = lax.dot_general(u_ref[...], h_ref[...], (((1,), (1,)), ((), ())),
                          preferred_element_type=F32)
    g_ref[...] = _gelu(act)

    for tc in range(tt // LANES):
        ts = slice(tc * LANES, (tc + 1) * LANES)

        def body(jq, carry):
            j0 = pl.multiple_of(jq * J_SUB, J_SUB)
            accs = [jnp.zeros((J_SUB, LANES), F32) for _ in range(I_BLK)]
            for hd in range(PEER_HEADS):
                s1p = s1p_ref[hd, pl.ds(j0, J_SUB), ts]
                bb = b_ref[hd, pl.ds(j0, J_SUB), ts]
                for il in range(I_BLK):
                    thr = ns0_ref[hd, il:il + 1, ts]
                    aa = a_ref[hd, il:il + 1, ts]
                    accs[il] = accs[il] + jnp.where(s1p >= thr, aa * bb, 0.0)
            for il in range(I_BLK):
                r0 = pl.multiple_of(il * PEER_NKEYS + j0, J_SUB)
                p_ref[pl.ds(r0, J_SUB), ts] = (accs[il] * g_ref[pl.ds(r0, J_SUB), ts]).astype(BF16)
            return carry

        lax.fori_loop(0, PEER_NKEYS // J_SUB, body, 0)

    acc_ref[...] += jnp.dot(vt_ref[...], p_ref[...], preferred_element_type=F32)

    @pl.when(e == pl.num_programs(1) - 1)
    def _():
        x = x_ref[...] + g2_ref[...] * acc_ref[...].T
        if final_norm:
            x = x * lax.rsqrt(jnp.mean(x * x, axis=-1, keepdims=True) + EPS) * fin_ref[...]
        o_ref[...] = x


def _peer(h2, u16, vt16, ns0, a, s1p, b, x, mod, final_gain, seq, final_norm, tt=512):
    t = h2.shape[0]
    tt = min(tt, seq)
    per_row = seq // tt
    ne = PEER_EXPERTS // E_BLK
    tok = lambda i, e: (i, 0)
    blk_i = pl.BlockSpec((PEER_HEADS, I_BLK, tt), lambda i, e: (0, e, i))
    full_j = pl.BlockSpec((PEER_HEADS, PEER_NKEYS, tt), lambda i, e: (0, 0, i))
    return pl.pallas_call(
        functools.partial(_peer_kernel, final_norm=final_norm),
        out_shape=jax.ShapeDtypeStruct((t, D_MODEL), F32),
        grid=(t // tt, ne),
        in_specs=[pl.BlockSpec((tt, D_MODEL), tok),
                  pl.BlockSpec((E_BLK, D_MODEL), lambda i, e: (e, 0)),
                  pl.BlockSpec((D_MODEL, E_BLK), lambda i, e: (0, e)),
                  blk_i, blk_i, full_j, full_j,
                  pl.BlockSpec((tt, D_MODEL), tok),
                  pl.BlockSpec((None, None, 1, D_MODEL), lambda i, e: (i // per_row, 5, 0, 0)),
                  pl.BlockSpec((1, D_MODEL), lambda i, e: (0, 0))],
        out_specs=pl.BlockSpec((tt, D_MODEL), tok),
        scratch_shapes=[pltpu.VMEM((E_BLK, tt), F32),
                        pltpu.VMEM((E_BLK, tt), BF16),
                        pltpu.VMEM((D_MODEL, tt), F32)],
        compiler_params=_params(("parallel", "arbitrary")),
        name="peer_dense",
    )(h2, u16, vt16, ns0, a, s1p, b, x, mod, final_gain.reshape(1, D_MODEL))


def _prepare_weights(w_in, w_spatial, b_spatial, dt_bias, a_log, d_skip, w_proj_a, w_proj_b, w_out,
                     w_query, sub_keys, expert_u, expert_v):
    pad = LANES - 2 * SSD_HEADS
    w = {
        "w_uv": w_in[:, :, OFF_UV:OFF_Z].astype(BF16),
        "w_z": w_in[:, :, OFF_Z:OFF_XBC].astype(BF16),
        "w_xbc": w_in[:, :, OFF_XBC:OFF_DT].astype(BF16),
        "w_dt": jnp.pad(w_in[:, :, OFF_DT:OFF_GATE], ((0, 0), (0, 0), (0, pad))).astype(BF16),
        "w_gate": w_in[:, :, OFF_GATE:OFF_END].astype(BF16),
        "ws": w_spatial.astype(BF16),
        "bs": jnp.broadcast_to(b_spatial[..., None], b_spatial.shape + (CHUNK,)),
        "dt_bias": jnp.pad(dt_bias.reshape(DEPTH, 1, 2 * SSD_HEADS), ((0, 0), (0, 0), (0, pad))),
        "a_log": jnp.pad(a_log.reshape(DEPTH, 1, 2 * SSD_HEADS), ((0, 0), (0, 0), (0, pad))),
        "d_skip": jnp.repeat(d_skip, SSD_HEAD_DIM, axis=1).reshape(DEPTH, 1, SSD_INNER),
        "wa": w_proj_a.astype(BF16),
        "wb": w_proj_b.astype(BF16),
        "wo": w_out.astype(BF16),
        "wq_t": jnp.swapaxes(w_query, 1, 2).astype(BF16),
        "keys": sub_keys.astype(BF16),
        "u": expert_u.astype(BF16),
        "vt": jnp.swapaxes(expert_v, 1, 2).astype(BF16),
    }
    return w


def _trunk(x, c, w_mod, b_mod, norm1_gain, norm2_gain, sgu_gain, conv_w, conv_b, ssd_gain, final_gain, w):
    bsz, seq, _ = x.shape
    t = bsz * seq
    mod_all = _modulation(c, w_mod, b_mod)
    xf = x.reshape(t, D_MODEL)
    for l in range(DEPTH):
        mod = mod_all[l]
        hn = _prenorm(xf.reshape(bsz, seq, D_MODEL), norm1_gain[l], mod, 0, 1).reshape(t, D_MODEL)
        uv = _matmul_act(hn, w["w_uv"][l], _gelu, BF16, 512, "proj_uv")
        z_act = _matmul_act(hn, w["w_z"][l], _silu, BF16, 512, "proj_z")
        xbc = _matmul_act(hn, w["w_xbc"][l], _identity, F32, 256, "proj_xbc")
        dt_raw = _matmul_act(hn, w["w_dt"][l], _identity, F32, 512, "proj_dt")
        gates = _matmul_act(hn, w["w_gate"][l], _sigmoid, BF16, 512, "proj_gate")
        pa = _sgu(uv, gates, sgu_gain[l], w["ws"][l], w["bs"][l], w["wa"][l])
        xbc_act = _conv_silu(xbc.reshape(bsz, seq, SSD_CONV_DIM), conv_w[l], conv_b[l])
        dt3 = dt_raw.reshape(bsz, seq, LANES)
        yf = _ssd_scan(xbc_act, dt3, w["dt_bias"][l], w["a_log"][l], reverse=False)
        yb = _ssd_scan(xbc_act, dt3, w["dt_bias"][l], w["a_log"][l], reverse=True)
        xf, h2 = _tail(yf.reshape(t, SSD_INNER), yb.reshape(t, SSD_INNER),
                       xbc_act.reshape(t, SSD_CONV_DIM), z_act, gates, pa, xf, mod,
                       w["d_skip"][l], ssd_gain[l], w["wb"][l], w["wo"][l], norm2_gain[l], seq)
        ns0, a, s1p, b = _route(h2, w["wq_t"][l], w["keys"][l])
        xf = _peer(h2, w["u"][l], w["vt"][l], ns0, a, s1p, b, xf, mod, final_gain, seq,
                   final_norm=(l == DEPTH - 1))
    return xf.reshape(bsz, seq, D_MODEL)


def kernel(x_prompt, x_sample, c_prompt, c_sample, w_mod, b_mod, norm1_gain, norm2_gain, w_in, sgu_gain, w_spatial, b_spatial, conv_w, conv_b, dt_bias, a_log, d_skip, ssd_gain, w_proj_a, w_proj_b, w_out, w_query, sub_keys, expert_u, expert_v, final_gain):
    w = _prepare_weights(w_in, w_spatial, b_spatial, dt_bias, a_log, d_skip, w_proj_a, w_proj_b,
                         w_out, w_query, sub_keys, expert_u, expert_v)
    args = (w_mod, b_mod, norm1_gain, norm2_gain, sgu_gain, conv_w, conv_b, ssd_gain, final_gain, w)
    y_prompt = _trunk(x_prompt, c_prompt, *args)
    y_sample = _trunk(x_sample, c_sample, *args)
    return (y_prompt, y_sample)
```

```python
import functools

import jax
import jax.numpy as jnp
from jax import lax
from jax.experimental import pallas as pl
from jax.experimental.pallas import tpu as pltpu

F32 = jnp.float32
BF16 = jnp.bfloat16

D_MODEL = 1024
DEPTH = 4
CHUNK = 128
SGU_WIDTH = 1024
SGU_GROUPS = 8
SSD_INNER = 2048
SSD_HEAD_DIM = 64
SSD_HEADS = 32
SSD_GROUPS = 4
SSD_STATE = 128
SSD_CONV = 5
SSD_CONV_DIM = 3072
PEER_HEADS = 8
PEER_NKEYS = 128
PEER_EXPERTS = PEER_NKEYS * PEER_NKEYS
PEER_HALF = 128
PEER_TOPK = 16
EPS = 1e-6

LANES = 128
MIB = 1024 * 1024
NEG_BIG = -3.0e38

OFF_UV, OFF_Z, OFF_XBC, OFF_DT, OFF_GATE, OFF_END = 0, 2048, 4096, 7168, 7232, 9280


def _params(semantics, vmem_mib=48):
    return pltpu.CompilerParams(dimension_semantics=semantics, vmem_limit_bytes=vmem_mib * MIB)


def _sigmoid(x):
    return 1.0 / (1.0 + jnp.exp(-x))


def _silu(x):
    return x * _sigmoid(x)


def _gelu(x):
    return 0.5 * x * (1.0 + jnp.tanh(0.7978845608028654 * (x + 0.044715 * (x * x * x))))


def _gelu_sigmoid_form(x):
    t = (x * x) * (-2.0 * 0.7978845608028654 * 0.044715) + (-2.0 * 0.7978845608028654)
    return x / (1.0 + jnp.exp(x * t))


def _bf16_pair_words(x):
    u = pltpu.bitcast(x.astype(BF16).astype(F32), jnp.uint32)
    return u | (u >> 16)


def _row_as_bf16_tile(words):
    return pltpu.bitcast(jnp.broadcast_to(words, (8, LANES)), BF16)


def _softplus(x):
    return jnp.maximum(x, 0.0) + jnp.log(1.0 + jnp.exp(-jnp.abs(x)))


def _identity(x):
    return x


def _mod_kernel(c_ref, w_ref, b_ref, o_ref):
    c = c_ref[...]
    o_ref[0] = jnp.dot(_silu(c), w_ref[0], preferred_element_type=F32,
                       precision=lax.Precision.HIGHEST) + b_ref[0]


def _modulation(c, w_mod, b_mod):
    bsz = c.shape[0]
    bp = -(-bsz // 8) * 8
    cp = jnp.pad(c, ((0, bp - bsz), (0, 0)))
    out = pl.pallas_call(
        _mod_kernel,
        out_shape=jax.ShapeDtypeStruct((DEPTH, bp, 6 * D_MODEL), F32),
        grid=(DEPTH, 6),
        in_specs=[pl.BlockSpec((bp, D_MODEL), lambda l, j: (0, 0)),
                  pl.BlockSpec((1, D_MODEL, D_MODEL), lambda l, j: (l, 0, j)),
                  pl.BlockSpec((1, 1, D_MODEL), lambda l, j: (l, 0, j))],
        out_specs=pl.BlockSpec((1, bp, D_MODEL), lambda l, j: (l, 0, j)),
        compiler_params=_params(("parallel", "parallel")),
        name="modulation",
    )(cp, w_mod, b_mod.reshape(DEPTH, 1, 6 * D_MODEL))
    return out[:, :bsz].reshape(DEPTH, bsz, 6, 1, D_MODEL)


def _prenorm_kernel(x_ref, gain_ref, sh_ref, sc_ref, o_ref):
    x = x_ref[0]
    y = x * lax.rsqrt(jnp.mean(x * x, axis=-1, keepdims=True) + EPS) * gain_ref[...]
    o_ref[0] = (y * (1.0 + sc_ref[...]) + sh_ref[...]).astype(o_ref.dtype)


def _prenorm(x, gain, mod, shift_idx, scale_idx, tl=512):
    bsz, seq, _ = x.shape
    tl = min(tl, seq)
    return pl.pallas_call(
        _prenorm_kernel,
        out_shape=jax.ShapeDtypeStruct((bsz, seq, D_MODEL), BF16),
        grid=(bsz, seq // tl),
        in_specs=[pl.BlockSpec((1, tl, D_MODEL), lambda b, i: (b, i, 0)),
                  pl.BlockSpec((1, D_MODEL), lambda b, i: (0, 0)),
                  pl.BlockSpec((None, None, 1, D_MODEL), lambda b, i: (b, shift_idx, 0, 0)),
                  pl.BlockSpec((None, None, 1, D_MODEL), lambda b, i: (b, scale_idx, 0, 0))],
        out_specs=pl.BlockSpec((1, tl, D_MODEL), lambda b, i: (b, i, 0)),
        compiler_params=_params(("parallel", "parallel")),
        name="prenorm",
    )(x, gain.reshape(1, D_MODEL), mod, mod)


def _matmul_kernel(a_ref, w_ref, o_ref, *, act):
    acc = jnp.dot(a_ref[...], w_ref[...], preferred_element_type=F32)
    o_ref[...] = act(acc).astype(o_ref.dtype)


def _matmul_act(a, w, act, out_dtype, tm, name):
    m, k = a.shape
    n = w.shape[1]
    tm = min(tm, m)
    return pl.pallas_call(
        functools.partial(_matmul_kernel, act=act),
        out_shape=jax.ShapeDtypeStruct((m, n), out_dtype),
        grid=(m // tm,),
        in_specs=[pl.BlockSpec((tm, k), lambda i: (i, 0)),
                  pl.BlockSpec((k, n), lambda i: (0, 0))],
        out_specs=pl.BlockSpec((tm, n), lambda i: (i, 0)),
        compiler_params=_params(("parallel",)),
        name=name,
    )(a, w)


def _sgu_kernel(uv_ref, ga_ref, gain_ref, ws_ref, bs_ref, wa_ref, o_ref, ya_ref):
    rows = uv_ref.shape[0]
    v = uv_ref[:, SGU_WIDTH:].astype(F32)
    vn = (v * lax.rsqrt(jnp.mean(v * v, axis=-1, keepdims=True) + EPS) * gain_ref[...]).astype(BF16)
    gdim = SGU_WIDTH // SGU_GROUPS
    for n in range(rows // CHUNK):
        r0 = n * CHUNK
        for g in range(SGU_GROUPS):
            c0 = g * gdim
            mixed = jnp.dot(ws_ref[g], vn[r0:r0 + CHUNK, c0:c0 + gdim],
                            preferred_element_type=F32) + bs_ref[g]
            u = uv_ref[r0:r0 + CHUNK, c0:c0 + gdim].astype(F32)
            ya_ref[r0:r0 + CHUNK, c0:c0 + gdim] = (u * mixed).astype(BF16)
    pa = jnp.dot(ya_ref[...], wa_ref[...], preferred_element_type=F32)
    o_ref[...] = (ga_ref[...].astype(F32) * pa).astype(o_ref.dtype)


def _sgu(uv, gates, sgu_gain, ws, bs_full, wa, ts=256):
    t = uv.shape[0]
    return pl.pallas_call(
        _sgu_kernel,
        out_shape=jax.ShapeDtypeStruct((t, D_MODEL), BF16),
        grid=(t // ts,),
        in_specs=[pl.BlockSpec((ts, 2 * SGU_WIDTH), lambda i: (i, 0)),
                  pl.BlockSpec((ts, D_MODEL), lambda i: (i, 0)),
                  pl.BlockSpec((1, SGU_WIDTH), lambda i: (0, 0)),
                  pl.BlockSpec((SGU_GROUPS, CHUNK, CHUNK), lambda i: (0, 0, 0)),
                  pl.BlockSpec((SGU_GROUPS, CHUNK, CHUNK), lambda i: (0, 0, 0)),
                  pl.BlockSpec((SGU_WIDTH, D_MODEL), lambda i: (0, 0))],
        out_specs=pl.BlockSpec((ts, D_MODEL), lambda i: (i, 0)),
        scratch_shapes=[pltpu.VMEM((ts, SGU_WIDTH), BF16)],
        compiler_params=_params(("parallel",)),
        name="sgu_proj_a",
    )(uv, gates, sgu_gain.reshape(1, SGU_WIDTH), ws, bs_full, wa)


def _conv_kernel(x_ref, prev_ref, next_ref, w_ref, b_ref, o_ref, ext_ref):
    i = pl.program_id(1)
    last = pl.num_programs(1) - 1
    tl = x_ref.shape[1]
    halo = prev_ref.shape[1]
    ext_ref[0:halo, :] = jnp.where(i > 0, prev_ref[0], 0.0)
    ext_ref[halo:halo + tl, :] = x_ref[0]
    ext_ref[halo + tl:, :] = jnp.where(i < last, next_ref[0], 0.0)
    acc = jnp.zeros((tl, x_ref.shape[2]), F32) + b_ref[...]
    for k in range(SSD_CONV):
        start = halo - SSD_CONV // 2 + k
        acc = acc + w_ref[k:k + 1, :] * ext_ref[start:start + tl, :]
    o_ref[0] = _silu(acc).astype(o_ref.dtype)


def _conv_silu(xbc, conv_w, conv_b, tl=512, tc=512):
    bsz, seq, ch = xbc.shape
    tl = min(tl, seq)
    halo = 8
    nblk = tl // halo
    last_blk = seq // halo - 1
    return pl.pallas_call(
        _conv_kernel,
        out_shape=jax.ShapeDtypeStruct((bsz, seq, ch), BF16),
        grid=(bsz, seq // tl, ch // tc),
        in_specs=[pl.BlockSpec((1, tl, tc), lambda b, i, c: (b, i, c)),
                  pl.BlockSpec((1, halo, tc), lambda b, i, c: (b, jnp.maximum(i * nblk - 1, 0), c)),
                  pl.BlockSpec((1, halo, tc), lambda b, i, c: (b, jnp.minimum((i + 1) * nblk, last_blk), c)),
                  pl.BlockSpec((SSD_CONV, tc), lambda b, i, c: (0, c)),
                  pl.BlockSpec((1, tc), lambda b, i, c: (0, c))],
        out_specs=pl.BlockSpec((1, tl, tc), lambda b, i, c: (b, i, c)),
        scratch_shapes=[pltpu.VMEM((tl + 2 * halo, tc), F32)],
        compiler_params=_params(("parallel", "parallel", "parallel")),
        name="conv_silu",
    )(xbc, xbc, xbc, conv_w, conv_b.reshape(1, ch))


def _ssd_kernel(xbc_ref, dt_ref, bias_ref, alog_ref, o_ref, state_ref, *, reverse):
    c = pl.program_id(1)

    @pl.when(c == 0)
    def _():
        state_ref[...] = jnp.zeros_like(state_ref)

    col0 = SSD_HEADS if reverse else 0
    row = lax.broadcasted_iota(jnp.int32, (CHUNK, CHUNK), 0)
    lane = lax.broadcasted_iota(jnp.int32, (CHUNK, CHUNK), 1)
    tri = (row <= lane) if reverse else (row >= lane)
    lo_half = lane < SSD_HEAD_DIM

    dt = _softplus(dt_ref[0] + bias_ref[...])
    adt = dt * (-jnp.exp(alog_ref[...]))
    cs = adt
    shift = 1
    while shift < CHUNK:
        if reverse:
            moved = pltpu.roll(cs, CHUNK - shift, axis=0)
            cs = cs + jnp.where(row < CHUNK - shift, moved, 0.0)
        else:
            moved = pltpu.roll(cs, shift, axis=0)
            cs = cs + jnp.where(row >= shift, moved, 0.0)
        shift *= 2
    cs_t = cs.T
    dt_t = dt.T
    end = 0 if reverse else CHUNK - 1
    cs_end = jnp.broadcast_to(cs_t[:, end:end + 1], (CHUNK, CHUNK))
    w_all = dt_t * jnp.exp(cs_end - cs_t)
    dec_all = jnp.exp(cs_end)

    hpg = SSD_HEADS // SSD_GROUPS
    for g in range(SSD_GROUPS):
        b_off = SSD_INNER + g * SSD_STATE
        c_off = SSD_INNER + SSD_GROUPS * SSD_STATE + g * SSD_STATE
        bg = xbc_ref[0, :, b_off:b_off + SSD_STATE]
        cg = xbc_ref[0, :, c_off:c_off + SSD_STATE]
        cb = lax.dot_general(cg, bg, (((1,), (1,)), ((), ())), preferred_element_type=F32)
        cg32 = cg.astype(F32)
        bg_t = bg.astype(F32).T
        for j in range(hpg // 2):
            pair = g * (hpg // 2) + j
            lhs_parts, lhs2_parts = [], []
            for k in range(2):
                col = col0 + 2 * pair + k
                lmat = jnp.broadcast_to(cs[:, col:col + 1], (CHUNK, CHUNK))
                decay = jnp.where(tri, jnp.exp(lmat - cs_t[col:col + 1, :]), 0.0)
                lhs_parts.append((cb * decay * dt_t[col:col + 1, :]).astype(BF16))
                lhs_parts.append((cg32 * jnp.exp(lmat)).astype(BF16))
                lhs2_parts.append((bg_t * w_all[col:col + 1, :]).astype(BF16))
            xs = xbc_ref[0, :, pair * LANES:(pair + 1) * LANES]
            zero = jnp.zeros_like(xs)
            x0 = jnp.where(lo_half, xs, zero)
            x1 = jnp.where(lo_half, zero, xs)
            st = state_ref[pair]
            st16 = st.astype(BF16)
            s0 = jnp.where(lo_half, st16, zero)
            s1 = jnp.where(lo_half, zero, st16)
            lhs = jnp.concatenate(lhs_parts, axis=1)
            rhs = jnp.concatenate([x0, s0, x1, s1], axis=0)
            o_ref[0, :, pair * LANES:(pair + 1) * LANES] = jnp.dot(
                lhs, rhs, preferred_element_type=F32).astype(o_ref.dtype)
            lhs2 = jnp.concatenate(lhs2_parts, axis=1)
            rhs2 = jnp.concatenate([x0, x1], axis=0)
            col_a = col0 + 2 * pair
            dec = jnp.where(lo_half, dec_all[col_a:col_a + 1, :], dec_all[col_a + 1:col_a + 2, :])
            state_ref[pair] = st * dec + jnp.dot(lhs2, rhs2, preferred_element_type=F32)


def _ssd_scan(xbc_act, dt_raw, dt_bias_row, alog_row, reverse):
    bsz, seq, _ = xbc_act.shape
    nc = seq // CHUNK
    if reverse:
        cmap = lambda b, c: (b, nc - 1 - c, 0)
    else:
        cmap = lambda b, c: (b, c, 0)
    return pl.pallas_call(
        functools.partial(_ssd_kernel, reverse=reverse),
        out_shape=jax.ShapeDtypeStruct((bsz, seq, SSD_INNER), BF16),
        grid=(bsz, nc),
        in_specs=[pl.BlockSpec((1, CHUNK, SSD_CONV_DIM), cmap),
                  pl.BlockSpec((1, CHUNK, LANES), cmap),
                  pl.BlockSpec((1, LANES), lambda b, c: (0, 0)),
                  pl.BlockSpec((1, LANES), lambda b, c: (0, 0))],
        out_specs=pl.BlockSpec((1, CHUNK, SSD_INNER), cmap),
        scratch_shapes=[pltpu.VMEM((SSD_HEADS // 2, SSD_STATE, LANES), F32)],
        compiler_params=_params(("parallel", "arbitrary")),
        name="ssd_bwd" if reverse else "ssd_fwd",
    )(xbc_act, dt_raw, dt_bias_row, alog_row)


def _tail_kernel(yf_ref, yb_ref, xs_ref, z_ref, gb_ref, pa_ref, x_ref, g1_ref, dskip_ref, gain_ref,
                 wb_ref, wo_ref, n2_ref, sh2_ref, sc2_ref, xo_ref, h2t_ref):
    y = yf_ref[...].astype(F32) + yb_ref[...].astype(F32) + dskip_ref[...] * xs_ref[...].astype(F32)
    y = y * z_ref[...].astype(F32)
    gw = SSD_INNER // SSD_GROUPS
    parts = []
    for g in range(SSD_GROUPS):
        yg = y[:, g * gw:(g + 1) * gw]
        parts.append(yg * lax.rsqrt(jnp.mean(yg * yg, axis=-1, keepdims=True) + EPS))
    yn = (jnp.concatenate(parts, axis=1) * gain_ref[...]).astype(BF16)
    pb = jnp.dot(yn, wb_ref[...], preferred_element_type=F32)
    merged = pa_ref[...].astype(F32) + gb_ref[...].astype(F32) * pb
    out = jnp.dot(merged.astype(BF16), wo_ref[...], preferred_element_type=F32)
    x = x_ref[...] + g1_ref[...] * out
    xo_ref[...] = x
    h = x * lax.rsqrt(jnp.mean(x * x, axis=-1, keepdims=True) + EPS) * n2_ref[...]
    h2t_ref[...] = (h * (1.0 + sc2_ref[...]) + sh2_ref[...]).T.astype(h2t_ref.dtype)


def _tail(yf, yb, xbc_act, z_act, gates, pa, x, mod, dskip_row, ssd_gain, wb, wo, norm2_gain, seq, tm=256):
    t = x.shape[0]
    tm = min(tm, seq)
    per_row = seq // tm
    row = lambda i: (i, 0)
    const = lambda i: (0, 0)
    modspec = lambda k: pl.BlockSpec((None, None, 1, D_MODEL), lambda i: (i // per_row, k, 0, 0))
    return pl.pallas_call(
        _tail_kernel,
        out_shape=(jax.ShapeDtypeStruct((t, D_MODEL), F32), jax.ShapeDtypeStruct((D_MODEL, t), BF16)),
        grid=(t // tm,),
        in_specs=[pl.BlockSpec((tm, SSD_INNER), row),
                  pl.BlockSpec((tm, SSD_INNER), row),
                  pl.BlockSpec((tm, SSD_INNER), row),
                  pl.BlockSpec((tm, SSD_INNER), row),
                  pl.BlockSpec((tm, D_MODEL), lambda i: (i, 1)),
                  pl.BlockSpec((tm, D_MODEL), row),
                  pl.BlockSpec((tm, D_MODEL), row),
                  modspec(2),
                  pl.BlockSpec((1, SSD_INNER), const),
                  pl.BlockSpec((1, SSD_INNER), const),
                  pl.BlockSpec((SSD_INNER, D_MODEL), const),
                  pl.BlockSpec((D_MODEL, D_MODEL), const),
                  pl.BlockSpec((1, D_MODEL), const),
                  modspec(3),
                  modspec(4)],
        out_specs=(pl.BlockSpec((tm, D_MODEL), row), pl.BlockSpec((D_MODEL, tm), lambda i: (0, i))),
        compiler_params=_params(("parallel",)),
        name="mix_tail",
    )(yf, yb, xbc_act, z_act, gates, pa, x, mod, dskip_row, ssd_gain.reshape(1, SSD_INNER), wb, wo,
      norm2_gain.reshape(1, D_MODEL), mod, mod)


CAND_PAIRS = [(a, b) for a in range(PEER_TOPK) for b in range(PEER_TOPK) if (a + 1) * (b + 1) <= PEER_TOPK]
RANK_OUTSIDE = float(2 * PEER_TOPK)


def _route_kernel(ht_ref, wq_ref, keys_ref, c0_ref, a_ref, r1_ref, b_ref, top_ref, s_ref):
    q_t = jnp.dot(wq_ref[...], ht_ref[...], preferred_element_type=F32)
    for hd in range(PEER_HEADS):
        for half in range(2):
            r0 = (hd * 2 + half) * PEER_HALF
            q = q_t[r0:r0 + PEER_HALF, :].astype(BF16)
            s = jnp.dot(keys_ref[half], q, preferred_element_type=F32)
            s_ref[half, hd] = s
            work = s
            rank = jnp.full(s.shape, RANK_OUTSIDE, F32)
            for r in range(PEER_TOPK):
                m = jnp.max(work, axis=0, keepdims=True)
                top_ref[half, r, hd:hd + 1, :] = m
                hit = work == m
                if half == 1:
                    rank = jnp.where(hit, float(r), rank)
                if r + 1 < PEER_TOPK:
                    work = jnp.where(hit, NEG_BIG, work)
            if half == 1:
                r1_ref[hd] = rank.astype(BF16)
    cands = [top_ref[0, a] + top_ref[1, b] for a, b in CAND_PAIRS]
    best = cands[0]
    work = list(cands)
    tau = best
    for r in range(PEER_TOPK):
        tau = work[0]
        for x in work[1:]:
            tau = jnp.maximum(tau, x)
        if r + 1 < PEER_TOPK:
            work = [jnp.where(x == tau, NEG_BIG, x) for x in work]
    z = jnp.zeros_like(best)
    for x in cands:
        z = z + jnp.where(x >= tau, jnp.exp(x - best), 0.0)
    inv_z = 1.0 / z
    for hd in range(PEER_HEADS):
        s0 = s_ref[0, hd]
        s1 = s_ref[1, hd]
        tau_h = tau[hd:hd + 1, :]
        cnt = jnp.zeros(s0.shape, F32)
        for b in range(PEER_TOPK):
            cnt = cnt + jnp.where(s0 + top_ref[1, b, hd:hd + 1, :] >= tau_h, 1.0, 0.0)
        c0_ref[hd] = _bf16_pair_words(cnt)
        a_ref[hd] = _bf16_pair_words(jnp.exp(s0 - top_ref[0, 0, hd:hd + 1, :]))
        b_ref[hd] = (jnp.exp(s1 - top_ref[1, 0, hd:hd + 1, :]) * inv_z[hd:hd + 1, :]).astype(BF16)


def _route(h2t, wq_t, keys, tt=256):
    t = h2t.shape[1]
    shp32 = jax.ShapeDtypeStruct((PEER_HEADS, PEER_NKEYS, t), jnp.uint32)
    shp16 = jax.ShapeDtypeStruct((PEER_HEADS, PEER_NKEYS, t), BF16)
    ospec = pl.BlockSpec((PEER_HEADS, PEER_NKEYS, tt), lambda i: (0, 0, i))
    return pl.pallas_call(
        _route_kernel,
        out_shape=(shp32, shp32, shp16, shp16),
        grid=(t // tt,),
        in_specs=[pl.BlockSpec((D_MODEL, tt), lambda i: (0, i)),
                  pl.BlockSpec((2 * PEER_HEADS * PEER_HALF, D_MODEL), lambda i: (0, 0)),
                  pl.BlockSpec((2, PEER_NKEYS, PEER_HALF), lambda i: (0, 0, 0))],
        out_specs=(ospec, ospec, ospec, ospec),
        scratch_shapes=[pltpu.VMEM((2, PEER_TOPK, PEER_HEADS, tt), F32),
                        pltpu.VMEM((2, PEER_HEADS, PEER_NKEYS, tt), F32)],
        compiler_params=_params(("parallel",)),
        name="peer_route",
    )(h2t, wq_t, keys)


E_BLK = 1024
I_BLK = E_BLK // PEER_NKEYS
E_SUB = 256
I_SUB = E_SUB // PEER_NKEYS
ROWS16 = 16


def _peer_kernel(ht_ref, u_ref, vt_ref, c0_ref, a_ref, r1_ref, b_ref, x_ref, g2_ref, fin_ref, o_ref,
                 act0_ref, act1_ref, p0_ref, p1_ref, acc_ref, *, final_norm):
    e = pl.program_id(1)
    tt = ht_ref.shape[1]
    t_sub = act0_ref.shape[1]
    n_jp = PEER_NKEYS // ROWS16
    act_refs = (act0_ref, act1_ref)
    p_refs = (p0_ref, p1_ref)
    units = [(sb, th) for sb in range(E_BLK // E_SUB) for th in range(tt // t_sub)]

    @pl.when(e == 0)
    def _():
        acc_ref[...] = jnp.zeros_like(acc_ref)

    def activation_matmul(n):
        sb, th = units[n]
        act_refs[n % 2][...] = jnp.dot(u_ref[sb * E_SUB:(sb + 1) * E_SUB, :],
                                       ht_ref[:, th * t_sub:(th + 1) * t_sub],
                                       preferred_element_type=F32)

    def gated_activations(n):
        sb, th = units[n]
        act_ref, p_ref = act_refs[n % 2], p_refs[n % 2]
        ils = [sb * I_SUB + k for k in range(I_SUB)]
        for tc in range(t_sub // LANES):
            ls = slice(tc * LANES, (tc + 1) * LANES)
            ts = slice(th * t_sub + tc * LANES, th * t_sub + (tc + 1) * LANES)
            accs = [[None] * n_jp for _ in ils]
            for hd in range(PEER_HEADS):
                cnt = [_row_as_bf16_tile(c0_ref[hd, il:il + 1, ts]) for il in ils]
                wgt = [_row_as_bf16_tile(a_ref[hd, il:il + 1, ts]) for il in ils]
                for jp in range(n_jp):
                    js = slice(jp * ROWS16, (jp + 1) * ROWS16)
                    r1 = r1_ref[hd, js, ts]
                    bb = b_ref[hd, js, ts]
                    for k in range(I_SUB):
                        term = jnp.minimum(jnp.maximum(cnt[k] - r1, 0.0), wgt[k]) * bb
                        accs[k][jp] = term if accs[k][jp] is None else accs[k][jp] + term
            for k in range(I_SUB):
                for jp in range(n_jp):
                    r0 = k * PEER_NKEYS + jp * ROWS16
                    gel = _gelu_sigmoid_form(act_ref[r0:r0 + ROWS16, ls].astype(BF16))
                    p_ref[r0:r0 + ROWS16, ls] = accs[k][jp] * gel

    def retrieval_matmul(n):
        sb, th = units[n]
        cols = slice(th * t_sub, (th + 1) * t_sub)
        acc_ref[:, cols] += jnp.dot(vt_ref[:, sb * E_SUB:(sb + 1) * E_SUB], p_refs[n % 2][...],
                                    preferred_element_type=F32)

    activation_matmul(0)
    for n in range(len(units)):
        if n + 1 < len(units):
            activation_matmul(n + 1)
        gated_activations(n)
        retrieval_matmul(n)

    @pl.when(e == pl.num_programs(1) - 1)
    def _():
        x = x_ref[...] + g2_ref[...] * acc_ref[...].T
        if final_norm:
            x = x * lax.rsqrt(jnp.mean(x * x, axis=-1, keepdims=True) + EPS) * fin_ref[...]
        o_ref[...] = x


def _peer(h2t, u16, vt16, c0, a, r1, b, x, mod, final_gain, seq, final_norm, tt=512):
    t = h2t.shape[1]
    tt = min(tt, seq)
    per_row = seq // tt
    ne = PEER_EXPERTS // E_BLK
    tok = lambda i, e: (i, 0)
    blk_i = pl.BlockSpec((PEER_HEADS, I_BLK, tt), lambda i, e: (0, e, i))
    full_j = pl.BlockSpec((PEER_HEADS, PEER_NKEYS, tt), lambda i, e: (0, 0, i))
    return pl.pallas_call(
        functools.partial(_peer_kernel, final_norm=final_norm),
        out_shape=jax.ShapeDtypeStruct((t, D_MODEL), F32),
        grid=(t // tt, ne),
        in_specs=[pl.BlockSpec((D_MODEL, tt), lambda i, e: (0, i)),
                  pl.BlockSpec((E_BLK, D_MODEL), lambda i, e: (e, 0)),
                  pl.BlockSpec((D_MODEL, E_BLK), lambda i, e: (0, e)),
                  blk_i, blk_i, full_j, full_j,
                  pl.BlockSpec((tt, D_MODEL), tok),
                  pl.BlockSpec((None, None, 1, D_MODEL), lambda i, e: (i // per_row, 5, 0, 0)),
                  pl.BlockSpec((1, D_MODEL), lambda i, e: (0, 0))],
        out_specs=pl.BlockSpec((tt, D_MODEL), tok),
        scratch_shapes=[pltpu.VMEM((E_SUB, tt), F32), pltpu.VMEM((E_SUB, tt), F32),
                        pltpu.VMEM((E_SUB, tt), BF16), pltpu.VMEM((E_SUB, tt), BF16),
                        pltpu.VMEM((D_MODEL, tt), F32)],
        compiler_params=_params(("parallel", "arbitrary")),
        name="peer_dense",
    )(h2t, u16, vt16, c0, a, r1, b, x, mod, final_gain.reshape(1, D_MODEL))


def _prepare_weights(w_in, w_spatial, b_spatial, dt_bias, a_log, d_skip, w_proj_a, w_proj_b, w_out,
                     w_query, sub_keys, expert_u, expert_v):
    pad = LANES - 2 * SSD_HEADS
    w = {
        "w_uv": w_in[:, :, OFF_UV:OFF_Z].astype(BF16),
        "w_z": w_in[:, :, OFF_Z:OFF_XBC].astype(BF16),
        "w_xbc": w_in[:, :, OFF_XBC:OFF_DT].astype(BF16),
        "w_dt": jnp.pad(w_in[:, :, OFF_DT:OFF_GATE], ((0, 0), (0, 0), (0, pad))).astype(BF16),
        "w_gate": w_in[:, :, OFF_GATE:OFF_END].astype(BF16),
        "ws": w_spatial.astype(BF16),
        "bs": jnp.broadcast_to(b_spatial[..., None], b_spatial.shape + (CHUNK,)),
        "dt_bias": jnp.pad(dt_bias.reshape(DEPTH, 1, 2 * SSD_HEADS), ((0, 0), (0, 0), (0, pad))),
        "a_log": jnp.pad(a_log.reshape(DEPTH, 1, 2 * SSD_HEADS), ((0, 0), (0, 0), (0, pad))),
        "d_skip": jnp.repeat(d_skip, SSD_HEAD_DIM, axis=1).reshape(DEPTH, 1, SSD_INNER),
        "wa": w_proj_a.astype(BF16),
        "wb": w_proj_b.astype(BF16),
        "wo": w_out.astype(BF16),
        "wq_t": jnp.swapaxes(w_query, 1, 2).astype(BF16),
        "keys": sub_keys.astype(BF16),
        "u": expert_u.astype(BF16),
        "vt": jnp.swapaxes(expert_v, 1, 2).astype(BF16),
    }
    return w


def _trunk(x, c, w_mod, b_mod, norm1_gain, norm2_gain, sgu_gain, conv_w, conv_b, ssd_gain, final_gain, w):
    bsz, seq, _ = x.shape
    t = bsz * seq
    mod_all = _modulation(c, w_mod, b_mod)
    xf = x.reshape(t, D_MODEL)
    for l in range(DEPTH):
        mod = mod_all[l]
        hn = _prenorm(xf.reshape(bsz, seq, D_MODEL), norm1_gain[l], mod, 0, 1).reshape(t, D_MODEL)
        uv = _matmul_act(hn, w["w_uv"][l], _gelu, BF16, 512, "proj_uv")
        z_act = _matmul_act(hn, w["w_z"][l], _silu, BF16, 512, "proj_z")
        xbc = _matmul_act(hn, w["w_xbc"][l], _identity, F32, 256, "proj_xbc")
        dt_raw = _matmul_act(hn, w["w_dt"][l], _identity, F32, 512, "proj_dt")
        gates = _matmul_act(hn, w["w_gate"][l], _sigmoid, BF16, 512, "proj_gate")
        pa = _sgu(uv, gates, sgu_gain[l], w["ws"][l], w["bs"][l], w["wa"][l])
        xbc_act = _conv_silu(xbc.reshape(bsz, seq, SSD_CONV_DIM), conv_w[l], conv_b[l])
        dt3 = dt_raw.reshape(bsz, seq, LANES)
        yf = _ssd_scan(xbc_act, dt3, w["dt_bias"][l], w["a_log"][l], reverse=False)
        yb = _ssd_scan(xbc_act, dt3, w["dt_bias"][l], w["a_log"][l], reverse=True)
        xf, h2t = _tail(yf.reshape(t, SSD_INNER), yb.reshape(t, SSD_INNER),
                       xbc_act.reshape(t, SSD_CONV_DIM), z_act, gates, pa, xf, mod,
                       w["d_skip"][l], ssd_gain[l], w["wb"][l], w["wo"][l], norm2_gain[l], seq)
        c0, a, r1, b = _route(h2t, w["wq_t"][l], w["keys"][l])
        xf = _peer(h2t, w["u"][l], w["vt"][l], c0, a, r1, b, xf, mod, final_gain, seq,
                   final_norm=(l == DEPTH - 1))
    return xf.reshape(bsz, seq, D_MODEL)


def kernel(x_prompt, x_sample, c_prompt, c_sample, w_mod, b_mod, norm1_gain, norm2_gain, w_in, sgu_gain, w_spatial, b_spatial, conv_w, conv_b, dt_bias, a_log, d_skip, ssd_gain, w_proj_a, w_proj_b, w_out, w_query, sub_keys, expert_u, expert_v, final_gain):
    w = _prepare_weights(w_in, w_spatial, b_spatial, dt_bias, a_log, d_skip, w_proj_a, w_proj_b,
                         w_out, w_query, sub_keys, expert_u, expert_v)
    args = (w_mod, b_mod, norm1_gain, norm2_gain, sgu_gain, conv_w, conv_b, ssd_gain, final_gain, w)
    y_prompt = _trunk(x_prompt, c_prompt, *args)
    y_sample = _trunk(x_sample, c_sample, *args)
    return (y_prompt, y_sample)
```

```python
import functools

import jax
import jax.numpy as jnp
from jax import lax
from jax.experimental import pallas as pl
from jax.experimental.pallas import tpu as pltpu

F32 = jnp.float32
BF16 = jnp.bfloat16

D_MODEL = 1024
DEPTH = 4
CHUNK = 128
SGU_WIDTH = 1024
SGU_GROUPS = 8
SSD_INNER = 2048
SSD_HEAD_DIM = 64
SSD_HEADS = 32
SSD_GROUPS = 4
SSD_STATE = 128
SSD_CONV = 5
SSD_CONV_DIM = 3072
PEER_HEADS = 8
PEER_NKEYS = 128
PEER_EXPERTS = PEER_NKEYS * PEER_NKEYS
PEER_HALF = 128
PEER_TOPK = 16
EPS = 1e-6

LANES = 128
MIB = 1024 * 1024
NEG_BIG = -3.0e38

OFF_UV, OFF_Z, OFF_XBC, OFF_DT, OFF_GATE, OFF_END = 0, 2048, 4096, 7168, 7232, 9280


def _params(semantics, vmem_mib=48):
    return pltpu.CompilerParams(dimension_semantics=semantics, vmem_limit_bytes=vmem_mib * MIB)


def _sigmoid(x):
    return 1.0 / (1.0 + jnp.exp(-x))


def _silu(x):
    return x * _sigmoid(x)


def _gelu(x):
    return 0.5 * x * (1.0 + jnp.tanh(0.7978845608028654 * (x + 0.044715 * (x * x * x))))


def _gelu_sigmoid_form(x):
    t = (x * x) * (-2.0 * 0.7978845608028654 * 0.044715) + (-2.0 * 0.7978845608028654)
    return x / (1.0 + jnp.exp(x * t))


def _bf16_pair_words(x):
    u = pltpu.bitcast(x.astype(BF16).astype(F32), jnp.uint32)
    return u | (u >> 16)


def _row_as_bf16_tile(words):
    return pltpu.bitcast(jnp.broadcast_to(words, (8, LANES)), BF16)


def _softplus(x):
    return jnp.maximum(x, 0.0) + jnp.log(1.0 + jnp.exp(-jnp.abs(x)))


def _identity(x):
    return x


def _mod_kernel(c_ref, w_ref, b_ref, o_ref):
    c = c_ref[...]
    o_ref[0] = jnp.dot(_silu(c), w_ref[0], preferred_element_type=F32,
                       precision=lax.Precision.HIGHEST) + b_ref[0]


def _modulation(c, w_mod, b_mod):
    bsz = c.shape[0]
    bp = -(-bsz // 8) * 8
    cp = jnp.pad(c, ((0, bp - bsz), (0, 0)))
    out = pl.pallas_call(
        _mod_kernel,
        out_shape=jax.ShapeDtypeStruct((DEPTH, bp, 6 * D_MODEL), F32),
        grid=(DEPTH, 6),
        in_specs=[pl.BlockSpec((bp, D_MODEL), lambda l, j: (0, 0)),
                  pl.BlockSpec((1, D_MODEL, D_MODEL), lambda l, j: (l, 0, j)),
                  pl.BlockSpec((1, 1, D_MODEL), lambda l, j: (l, 0, j))],
        out_specs=pl.BlockSpec((1, bp, D_MODEL), lambda l, j: (l, 0, j)),
        compiler_params=_params(("parallel", "parallel")),
        name="modulation",
    )(cp, w_mod, b_mod.reshape(DEPTH, 1, 6 * D_MODEL))
    return out[:, :bsz].reshape(DEPTH, bsz, 6, 1, D_MODEL)


def _prenorm_kernel(x_ref, gain_ref, sh_ref, sc_ref, o_ref):
    x = x_ref[0]
    y = x * lax.rsqrt(jnp.mean(x * x, axis=-1, keepdims=True) + EPS) * gain_ref[...]
    o_ref[0] = (y * (1.0 + sc_ref[...]) + sh_ref[...]).astype(o_ref.dtype)


def _prenorm(x, gain, mod, shift_idx, scale_idx, tl=512):
    bsz, seq, _ = x.shape
    tl = min(tl, seq)
    return pl.pallas_call(
        _prenorm_kernel,
        out_shape=jax.ShapeDtypeStruct((bsz, seq, D_MODEL), BF16),
        grid=(bsz, seq // tl),
        in_specs=[pl.BlockSpec((1, tl, D_MODEL), lambda b, i: (b, i, 0)),
                  pl.BlockSpec((1, D_MODEL), lambda b, i: (0, 0)),
                  pl.BlockSpec((None, None, 1, D_MODEL), lambda b, i: (b, shift_idx, 0, 0)),
                  pl.BlockSpec((None, None, 1, D_MODEL), lambda b, i: (b, scale_idx, 0, 0))],
        out_specs=pl.BlockSpec((1, tl, D_MODEL), lambda b, i: (b, i, 0)),
        compiler_params=_params(("parallel", "parallel")),
        name="prenorm",
    )(x, gain.reshape(1, D_MODEL), mod, mod)


def _matmul_kernel(a_ref, w_ref, o_ref, *, act):
    acc = jnp.dot(a_ref[...], w_ref[...], preferred_element_type=F32)
    o_ref[...] = act(acc).astype(o_ref.dtype)


def _matmul_act(a, w, act, out_dtype, tm, name):
    m, k = a.shape
    n = w.shape[1]
    tm = min(tm, m)
    return pl.pallas_call(
        functools.partial(_matmul_kernel, act=act),
        out_shape=jax.ShapeDtypeStruct((m, n), out_dtype),
        grid=(m // tm,),
        in_specs=[pl.BlockSpec((tm, k), lambda i: (i, 0)),
                  pl.BlockSpec((k, n), lambda i: (0, 0))],
        out_specs=pl.BlockSpec((tm, n), lambda i: (i, 0)),
        compiler_params=_params(("parallel",)),
        name=name,
    )(a, w)


def _sgu_kernel(uv_ref, ga_ref, gain_ref, ws_ref, bs_ref, wa_ref, o_ref, ya_ref):
    rows = uv_ref.shape[0]
    v = uv_ref[:, SGU_WIDTH:].astype(F32)
    vn = (v * lax.rsqrt(jnp.mean(v * v, axis=-1, keepdims=True) + EPS) * gain_ref[...]).astype(BF16)
    gdim = SGU_WIDTH // SGU_GROUPS
    for n in range(rows // CHUNK):
        r0 = n * CHUNK
        for g in range(SGU_GROUPS):
            c0 = g * gdim
            mixed = jnp.dot(ws_ref[g], vn[r0:r0 + CHUNK, c0:c0 + gdim],
                            preferred_element_type=F32) + bs_ref[g]
            u = uv_ref[r0:r0 + CHUNK, c0:c0 + gdim].astype(F32)
            ya_ref[r0:r0 + CHUNK, c0:c0 + gdim] = (u * mixed).astype(BF16)
    pa = jnp.dot(ya_ref[...], wa_ref[...], preferred_element_type=F32)
    o_ref[...] = (ga_ref[...].astype(F32) * pa).astype(o_ref.dtype)


def _sgu(uv, gates, sgu_gain, ws, bs_full, wa, ts=256):
    t = uv.shape[0]
    return pl.pallas_call(
        _sgu_kernel,
        out_shape=jax.ShapeDtypeStruct((t, D_MODEL), BF16),
        grid=(t // ts,),
        in_specs=[pl.BlockSpec((ts, 2 * SGU_WIDTH), lambda i: (i, 0)),
                  pl.BlockSpec((ts, D_MODEL), lambda i: (i, 0)),
                  pl.BlockSpec((1, SGU_WIDTH), lambda i: (0, 0)),
                  pl.BlockSpec((SGU_GROUPS, CHUNK, CHUNK), lambda i: (0, 0, 0)),
                  pl.BlockSpec((SGU_GROUPS, CHUNK, CHUNK), lambda i: (0, 0, 0)),
                  pl.BlockSpec((SGU_WIDTH, D_MODEL), lambda i: (0, 0))],
        out_specs=pl.BlockSpec((ts, D_MODEL), lambda i: (i, 0)),
        scratch_shapes=[pltpu.VMEM((ts, SGU_WIDTH), BF16)],
        compiler_params=_params(("parallel",)),
        name="sgu_proj_a",
    )(uv, gates, sgu_gain.reshape(1, SGU_WIDTH), ws, bs_full, wa)


def _conv_kernel(x_ref, prev_ref, next_ref, w_ref, b_ref, o_ref, ext_ref):
    i = pl.program_id(1)
    last = pl.num_programs(1) - 1
    tl = x_ref.shape[1]
    halo = prev_ref.shape[1]
    ext_ref[0:halo, :] = jnp.where(i > 0, prev_ref[0], 0.0)
    ext_ref[halo:halo + tl, :] = x_ref[0]
    ext_ref[halo + tl:, :] = jnp.where(i < last, next_ref[0], 0.0)
    acc = jnp.zeros((tl, x_ref.shape[2]), F32) + b_ref[...]
    for k in range(SSD_CONV):
        start = halo - SSD_CONV // 2 + k
        acc = acc + w_ref[k:k + 1, :] * ext_ref[start:start + tl, :]
    o_ref[0] = _silu(acc).astype(o_ref.dtype)


def _conv_silu(xbc, conv_w, conv_b, tl=512, tc=512):
    bsz, seq, ch = xbc.shape
    tl = min(tl, seq)
    halo = 8
    nblk = tl // halo
    last_blk = seq // halo - 1
    return pl.pallas_call(
        _conv_kernel,
        out_shape=jax.ShapeDtypeStruct((bsz, seq, ch), BF16),
        grid=(bsz, seq // tl, ch // tc),
        in_specs=[pl.BlockSpec((1, tl, tc), lambda b, i, c: (b, i, c)),
                  pl.BlockSpec((1, halo, tc), lambda b, i, c: (b, jnp.maximum(i * nblk - 1, 0), c)),
                  pl.BlockSpec((1, halo, tc), lambda b, i, c: (b, jnp.minimum((i + 1) * nblk, last_blk), c)),
                  pl.BlockSpec((SSD_CONV, tc), lambda b, i, c: (0, c)),
                  pl.BlockSpec((1, tc), lambda b, i, c: (0, c))],
        out_specs=pl.BlockSpec((1, tl, tc), lambda b, i, c: (b, i, c)),
        scratch_shapes=[pltpu.VMEM((tl + 2 * halo, tc), F32)],
        compiler_params=_params(("parallel", "parallel", "parallel")),
        name="conv_silu",
    )(xbc, xbc, xbc, conv_w, conv_b.reshape(1, ch))


def _ssd_kernel(xbc_ref, dt_ref, bias_ref, alog_ref, o_ref, state_ref, *, reverse):
    c = pl.program_id(1)

    @pl.when(c == 0)
    def _():
        state_ref[...] = jnp.zeros_like(state_ref)

    col0 = SSD_HEADS if reverse else 0
    row = lax.broadcasted_iota(jnp.int32, (CHUNK, CHUNK), 0)
    lane = lax.broadcasted_iota(jnp.int32, (CHUNK, CHUNK), 1)
    tri = (row <= lane) if reverse else (row >= lane)
    lo_half = lane < SSD_HEAD_DIM

    dt = _softplus(dt_ref[0] + bias_ref[...])
    adt = dt * (-jnp.exp(alog_ref[...]))
    cs = adt
    shift = 1
    while shift < CHUNK:
        if reverse:
            moved = pltpu.roll(cs, CHUNK - shift, axis=0)
            cs = cs + jnp.where(row < CHUNK - shift, moved, 0.0)
        else:
            moved = pltpu.roll(cs, shift, axis=0)
            cs = cs + jnp.where(row >= shift, moved, 0.0)
        shift *= 2
    cs_t = cs.T
    dt_t = dt.T
    end = 0 if reverse else CHUNK - 1
    cs_end = jnp.broadcast_to(cs_t[:, end:end + 1], (CHUNK, CHUNK))
    w_all = dt_t * jnp.exp(cs_end - cs_t)
    dec_all = jnp.exp(cs_end)

    hpg = SSD_HEADS // SSD_GROUPS
    for g in range(SSD_GROUPS):
        b_off = SSD_INNER + g * SSD_STATE
        c_off = SSD_INNER + SSD_GROUPS * SSD_STATE + g * SSD_STATE
        bg = xbc_ref[0, :, b_off:b_off + SSD_STATE]
        cg = xbc_ref[0, :, c_off:c_off + SSD_STATE]
        cb = lax.dot_general(cg, bg, (((1,), (1,)), ((), ())), preferred_element_type=F32)
        cg32 = cg.astype(F32)
        bg_t = bg.astype(F32).T
        for j in range(hpg // 2):
            pair = g * (hpg // 2) + j
            lhs_parts, lhs2_parts = [], []
            for k in range(2):
                col = col0 + 2 * pair + k
                lmat = jnp.broadcast_to(cs[:, col:col + 1], (CHUNK, CHUNK))
                decay = jnp.where(tri, jnp.exp(lmat - cs_t[col:col + 1, :]), 0.0)
                lhs_parts.append((cb * decay * dt_t[col:col + 1, :]).astype(BF16))
                lhs_parts.append((cg32 * jnp.exp(lmat)).astype(BF16))
                lhs2_parts.append((bg_t * w_all[col:col + 1, :]).astype(BF16))
            xs = xbc_ref[0, :, pair * LANES:(pair + 1) * LANES]
            zero = jnp.zeros_like(xs)
            x0 = jnp.where(lo_half, xs, zero)
            x1 = jnp.where(lo_half, zero, xs)
            st = state_ref[pair]
            st16 = st.astype(BF16)
            s0 = jnp.where(lo_half, st16, zero)
            s1 = jnp.where(lo_half, zero, st16)
            lhs = jnp.concatenate(lhs_parts, axis=1)
            rhs = jnp.concatenate([x0, s0, x1, s1], axis=0)
            o_ref[0, :, pair * LANES:(pair + 1) * LANES] = jnp.dot(
                lhs, rhs, preferred_element_type=F32).astype(o_ref.dtype)
            lhs2 = jnp.concatenate(lhs2_parts, axis=1)
            rhs2 = jnp.concatenate([x0, x1], axis=0)
            col_a = col0 + 2 * pair
            dec = jnp.where(lo_half, dec_all[col_a:col_a + 1, :], dec_all[col_a + 1:col_a + 2, :])
            state_ref[pair] = st * dec + jnp.dot(lhs2, rhs2, preferred_element_type=F32)


def _ssd_scan(xbc_act, dt_raw, dt_bias_row, alog_row, reverse):
    bsz, seq, _ = xbc_act.shape
    nc = seq // CHUNK
    if reverse:
        cmap = lambda b, c: (b, nc - 1 - c, 0)
    else:
        cmap = lambda b, c: (b, c, 0)
    return pl.pallas_call(
        functools.partial(_ssd_kernel, reverse=reverse),
        out_shape=jax.ShapeDtypeStruct((bsz, seq, SSD_INNER), BF16),
        grid=(bsz, nc),
        in_specs=[pl.BlockSpec((1, CHUNK, SSD_CONV_DIM), cmap),
                  pl.BlockSpec((1, CHUNK, LANES), cmap),
                  pl.BlockSpec((1, LANES), lambda b, c: (0, 0)),
                  pl.BlockSpec((1, LANES), lambda b, c: (0, 0))],
        out_specs=pl.BlockSpec((1, CHUNK, SSD_INNER), cmap),
        scratch_shapes=[pltpu.VMEM((SSD_HEADS // 2, SSD_STATE, LANES), F32)],
        compiler_params=_params(("parallel", "arbitrary")),
        name="ssd_bwd" if reverse else "ssd_fwd",
    )(xbc_act, dt_raw, dt_bias_row, alog_row)


def _tail_kernel(yf_ref, yb_ref, xs_ref, z_ref, gb_ref, pa_ref, x_ref, g1_ref, dskip_ref, gain_ref,
                 wb_ref, wo_ref, n2_ref, sh2_ref, sc2_ref, xo_ref, h2t_ref):
    y = yf_ref[...].astype(F32) + yb_ref[...].astype(F32) + dskip_ref[...] * xs_ref[...].astype(F32)
    y = y * z_ref[...].astype(F32)
    gw = SSD_INNER // SSD_GROUPS
    parts = []
    for g in range(SSD_GROUPS):
        yg = y[:, g * gw:(g + 1) * gw]
        parts.append(yg * lax.rsqrt(jnp.mean(yg * yg, axis=-1, keepdims=True) + EPS))
    yn = (jnp.concatenate(parts, axis=1) * gain_ref[...]).astype(BF16)
    pb = jnp.dot(yn, wb_ref[...], preferred_element_type=F32)
    merged = pa_ref[...].astype(F32) + gb_ref[...].astype(F32) * pb
    out = jnp.dot(merged.astype(BF16), wo_ref[...], preferred_element_type=F32)
    x = x_ref[...] + g1_ref[...] * out
    xo_ref[...] = x
    h = x * lax.rsqrt(jnp.mean(x * x, axis=-1, keepdims=True) + EPS) * n2_ref[...]
    h2t_ref[...] = (h * (1.0 + sc2_ref[...]) + sh2_ref[...]).T.astype(h2t_ref.dtype)


def _tail(yf, yb, xbc_act, z_act, gates, pa, x, mod, dskip_row, ssd_gain, wb, wo, norm2_gain, seq, tm=256):
    t = x.shape[0]
    tm = min(tm, seq)
    per_row = seq // tm
    row = lambda i: (i, 0)
    const = lambda i: (0, 0)
    modspec = lambda k: pl.BlockSpec((None, None, 1, D_MODEL), lambda i: (i // per_row, k, 0, 0))
    return pl.pallas_call(
        _tail_kernel,
        out_shape=(jax.ShapeDtypeStruct((t, D_MODEL), F32), jax.ShapeDtypeStruct((D_MODEL, t), BF16)),
        grid=(t // tm,),
        in_specs=[pl.BlockSpec((tm, SSD_INNER), row),
                  pl.BlockSpec((tm, SSD_INNER), row),
                  pl.BlockSpec((tm, SSD_INNER), row),
                  pl.BlockSpec((tm, SSD_INNER), row),
                  pl.BlockSpec((tm, D_MODEL), lambda i: (i, 1)),
                  pl.BlockSpec((tm, D_MODEL), row),
                  pl.BlockSpec((tm, D_MODEL), row),
                  modspec(2),
                  pl.BlockSpec((1, SSD_INNER), const),
                  pl.BlockSpec((1, SSD_INNER), const),
                  pl.BlockSpec((SSD_INNER, D_MODEL), const),
                  pl.BlockSpec((D_MODEL, D_MODEL), const),
                  pl.BlockSpec((1, D_MODEL), const),
                  modspec(3),
                  modspec(4)],
        out_specs=(pl.BlockSpec((tm, D_MODEL), row), pl.BlockSpec((D_MODEL, tm), lambda i: (0, i))),
        compiler_params=_params(("parallel",)),
        name="mix_tail",
    )(yf, yb, xbc_act, z_act, gates, pa, x, mod, dskip_row, ssd_gain.reshape(1, SSD_INNER), wb, wo,
      norm2_gain.reshape(1, D_MODEL), mod, mod)


CAND_PAIRS = [(a, b) for a in range(PEER_TOPK) for b in range(PEER_TOPK) if (a + 1) * (b + 1) <= PEER_TOPK]
RANK_OUTSIDE = float(2 * PEER_TOPK)
SUBLANES = 8


def _sorted_desc(cols):
    c = list(cols)
    n = len(c)
    k = 2
    while k <= n:
        j = k // 2
        while j >= 1:
            for i in range(n):
                l = i ^ j
                if l > i:
                    hi, lo = jnp.maximum(c[i], c[l]), jnp.minimum(c[i], c[l])
                    c[i], c[l] = (hi, lo) if (i & k) == 0 else (lo, hi)
            j //= 2
        k *= 2
    return c


def _top_values(s):
    c = _sorted_desc([s[v * SUBLANES:(v + 1) * SUBLANES, :] for v in range(PEER_NKEYS // SUBLANES)])
    tops = []
    for r in range(PEER_TOPK):
        m = jnp.max(c[0], axis=0, keepdims=True)
        tops.append(m)
        depth = PEER_TOPK - 1 - r
        if depth:
            hit = c[0] == m
            c = [jnp.where(hit, c[q + 1], c[q]) for q in range(depth)]
    return tops


def _count_at_least(x, thresholds):
    cnt = jnp.zeros(x.shape, F32)
    for n, th in enumerate(thresholds):
        cnt = jnp.where(x >= th, float(n + 1), cnt)
    return cnt


def _route_kernel(ht_ref, wq_ref, keys_ref, c0_ref, a_ref, r1_ref, b_ref, top_ref, s_ref):
    q_t = jnp.dot(wq_ref[...], ht_ref[...], preferred_element_type=F32)
    for hd in range(PEER_HEADS):
        for half in range(2):
            r0 = (hd * 2 + half) * PEER_HALF
            q = q_t[r0:r0 + PEER_HALF, :].astype(BF16)
            s = jnp.dot(keys_ref[half], q, preferred_element_type=F32)
            s_ref[half, hd] = s
            for r, m in enumerate(_top_values(s)):
                top_ref[half, r, hd:hd + 1, :] = m
    cands = [top_ref[0, a] + top_ref[1, b] for a, b in CAND_PAIRS]
    best = cands[0]
    work = list(cands)
    kth = best
    for r in range(PEER_TOPK + 1):
        prev = kth
        kth = work[0]
        for x in work[1:]:
            kth = jnp.maximum(kth, x)
        if r < PEER_TOPK:
            work = [jnp.where(x == kth, NEG_BIG, x) for x in work]
    tau = 0.5 * (prev + kth)
    z = jnp.zeros_like(best)
    for x in cands:
        z = z + jnp.where(x >= tau, jnp.exp(x - best), 0.0)
    inv_z = 1.0 / z
    for hd in range(PEER_HEADS):
        s0 = s_ref[0, hd]
        s1 = s_ref[1, hd]
        tau_h = tau[hd:hd + 1, :]
        tops1 = [top_ref[1, b, hd:hd + 1, :] for b in range(PEER_TOPK)]
        floor0 = top_ref[0, PEER_TOPK - 1, hd:hd + 1, :]
        c0_ref[hd] = _bf16_pair_words(_count_at_least(s0, [jnp.maximum(tau_h - t, floor0) for t in tops1]))
        reached = _count_at_least(s1, tops1[::-1])
        r1_ref[hd] = jnp.where(reached > 0.0, float(PEER_TOPK) - reached, RANK_OUTSIDE).astype(BF16)
        a_ref[hd] = _bf16_pair_words(jnp.exp(s0 - top_ref[0, 0, hd:hd + 1, :]))
        b_ref[hd] = (jnp.exp(s1 - tops1[0]) * inv_z[hd:hd + 1, :]).astype(BF16)


def _route(h2t, wq_t, keys, tt=256):
    t = h2t.shape[1]
    shp32 = jax.ShapeDtypeStruct((PEER_HEADS, PEER_NKEYS, t), jnp.uint32)
    shp16 = jax.ShapeDtypeStruct((PEER_HEADS, PEER_NKEYS, t), BF16)
    ospec = pl.BlockSpec((PEER_HEADS, PEER_NKEYS, tt), lambda i: (0, 0, i))
    return pl.pallas_call(
        _route_kernel,
        out_shape=(shp32, shp32, shp16, shp16),
        grid=(t // tt,),
        in_specs=[pl.BlockSpec((D_MODEL, tt), lambda i: (0, i)),
                  pl.BlockSpec((2 * PEER_HEADS * PEER_HALF, D_MODEL), lambda i: (0, 0)),
                  pl.BlockSpec((2, PEER_NKEYS, PEER_HALF), lambda i: (0, 0, 0))],
        out_specs=(ospec, ospec, ospec, ospec),
        scratch_shapes=[pltpu.VMEM((2, PEER_TOPK, PEER_HEADS, tt), F32),
                        pltpu.VMEM((2, PEER_HEADS, PEER_NKEYS, tt), F32)],
        compiler_params=_params(("parallel",)),
        name="peer_route",
    )(h2t, wq_t, keys)


E_BLK = 2048
I_BLK = E_BLK // PEER_NKEYS
E_SUB = 256
I_SUB = E_SUB // PEER_NKEYS
ROWS16 = 16


def _peer_kernel(ht_ref, u_ref, vt_ref, c0_ref, a_ref, r1_ref, b_ref, x_ref, g2_ref, gain_ref, sh_ref, sc_ref,
                 *refs, final_norm):
    n_out = 1 if final_norm else 2
    o_ref = refs[0]
    act0_ref, act1_ref, p0_ref, p1_ref, acc_ref = refs[n_out:]
    e = pl.program_id(1)
    tt = ht_ref.shape[1]
    t_sub = act0_ref.shape[1]
    n_jp = PEER_NKEYS // ROWS16
    act_refs = (act0_ref, act1_ref)
    p_refs = (p0_ref, p1_ref)
    units = [(sb, th) for sb in range(E_BLK // E_SUB) for th in range(tt // t_sub)]

    @pl.when(e == 0)
    def _():
        acc_ref[...] = jnp.zeros_like(acc_ref)

    def activation_matmul(n):
        sb, th = units[n]
        act_refs[n % 2][...] = jnp.dot(u_ref[sb * E_SUB:(sb + 1) * E_SUB, :],
                                       ht_ref[:, th * t_sub:(th + 1) * t_sub],
                                       preferred_element_type=F32)

    def gated_activations(n):
        sb, th = units[n]
        act_ref, p_ref = act_refs[n % 2], p_refs[n % 2]
        ils = [sb * I_SUB + k for k in range(I_SUB)]
        for tc in range(t_sub // LANES):
            ls = slice(tc * LANES, (tc + 1) * LANES)
            ts = slice(th * t_sub + tc * LANES, th * t_sub + (tc + 1) * LANES)
            accs = [[None] * n_jp for _ in ils]
            for hd in range(PEER_HEADS):
                cnt = [_row_as_bf16_tile(c0_ref[hd, il:il + 1, ts]) for il in ils]
                wgt = [_row_as_bf16_tile(a_ref[hd, il:il + 1, ts]) for il in ils]
                for jp in range(n_jp):
                    js = slice(jp * ROWS16, (jp + 1) * ROWS16)
                    r1 = r1_ref[hd, js, ts]
                    bb = b_ref[hd, js, ts]
                    for k in range(I_SUB):
                        term = jnp.minimum(jnp.maximum(cnt[k] - r1, 0.0), wgt[k]) * bb
                        accs[k][jp] = term if accs[k][jp] is None else accs[k][jp] + term
            for k in range(I_SUB):
                for jp in range(n_jp):
                    r0 = k * PEER_NKEYS + jp * ROWS16
                    gel = _gelu_sigmoid_form(act_ref[r0:r0 + ROWS16, ls].astype(BF16))
                    p_ref[r0:r0 + ROWS16, ls] = accs[k][jp] * gel

    def retrieval_matmul(n):
        sb, th = units[n]
        cols = slice(th * t_sub, (th + 1) * t_sub)
        acc_ref[:, cols] += jnp.dot(vt_ref[:, sb * E_SUB:(sb + 1) * E_SUB], p_refs[n % 2][...],
                                    preferred_element_type=F32)

    activation_matmul(0)
    for n in range(len(units)):
        if n + 1 < len(units):
            activation_matmul(n + 1)
        gated_activations(n)
        retrieval_matmul(n)

    @pl.when(e == pl.num_programs(1) - 1)
    def _():
        x = x_ref[...] + g2_ref[...] * acc_ref[...].T
        y = x * lax.rsqrt(jnp.mean(x * x, axis=-1, keepdims=True) + EPS) * gain_ref[...]
        if final_norm:
            o_ref[...] = y
        else:
            o_ref[...] = x
            refs[1][...] = (y * (1.0 + sc_ref[...]) + sh_ref[...]).astype(BF16)


def _peer(h2t, u16, vt16, c0, a, r1, b, x, mod, gain, mod_next, seq, final_norm, tt=512):
    t = h2t.shape[1]
    tt = min(tt, seq)
    per_row = seq // tt
    ne = PEER_EXPERTS // E_BLK
    tok = lambda i, e: (i, 0)
    blk_i = pl.BlockSpec((PEER_HEADS, I_BLK, tt), lambda i, e: (0, e, i))
    full_j = pl.BlockSpec((PEER_HEADS, PEER_NKEYS, tt), lambda i, e: (0, 0, i))
    modspec = lambda tbl_idx: pl.BlockSpec((None, None, 1, D_MODEL), lambda i, e: (i // per_row, tbl_idx, 0, 0))
    x_shape = jax.ShapeDtypeStruct((t, D_MODEL), F32)
    x_spec = pl.BlockSpec((tt, D_MODEL), tok)
    return pl.pallas_call(
        functools.partial(_peer_kernel, final_norm=final_norm),
        out_shape=x_shape if final_norm else (x_shape, jax.ShapeDtypeStruct((t, D_MODEL), BF16)),
        grid=(t // tt, ne),
        in_specs=[pl.BlockSpec((D_MODEL, tt), lambda i, e: (0, i)),
                  pl.BlockSpec((E_BLK, D_MODEL), lambda i, e: (e, 0)),
                  pl.BlockSpec((D_MODEL, E_BLK), lambda i, e: (0, e)),
                  blk_i, blk_i, full_j, full_j,
                  x_spec,
                  modspec(5),
                  pl.BlockSpec((1, D_MODEL), lambda i, e: (0, 0)),
                  modspec(0),
                  modspec(1)],
        out_specs=x_spec if final_norm else (x_spec, x_spec),
        scratch_shapes=[pltpu.VMEM((E_SUB, tt), F32), pltpu.VMEM((E_SUB, tt), F32),
                        pltpu.VMEM((E_SUB, tt), BF16), pltpu.VMEM((E_SUB, tt), BF16),
                        pltpu.VMEM((D_MODEL, tt), F32)],
        compiler_params=_params(("parallel", "arbitrary")),
        name="peer_dense",
    )(h2t, u16, vt16, c0, a, r1, b, x, mod, gain.reshape(1, D_MODEL), mod_next, mod_next)


def _prepare_weights(w_in, w_spatial, b_spatial, dt_bias, a_log, d_skip, w_proj_a, w_proj_b, w_out,
                     w_query, sub_keys, expert_u, expert_v):
    pad = LANES - 2 * SSD_HEADS
    w = {
        "w_uv": w_in[:, :, OFF_UV:OFF_Z].astype(BF16),
        "w_z": w_in[:, :, OFF_Z:OFF_XBC].astype(BF16),
        "w_xbc": w_in[:, :, OFF_XBC:OFF_DT].astype(BF16),
        "w_dt": jnp.pad(w_in[:, :, OFF_DT:OFF_GATE], ((0, 0), (0, 0), (0, pad))).astype(BF16),
        "w_gate": w_in[:, :, OFF_GATE:OFF_END].astype(BF16),
        "ws": w_spatial.astype(BF16),
        "bs": jnp.broadcast_to(b_spatial[..., None], b_spatial.shape + (CHUNK,)),
        "dt_bias": jnp.pad(dt_bias.reshape(DEPTH, 1, 2 * SSD_HEADS), ((0, 0), (0, 0), (0, pad))),
        "a_log": jnp.pad(a_log.reshape(DEPTH, 1, 2 * SSD_HEADS), ((0, 0), (0, 0), (0, pad))),
        "d_skip": jnp.repeat(d_skip, SSD_HEAD_DIM, axis=1).reshape(DEPTH, 1, SSD_INNER),
        "wa": w_proj_a.astype(BF16),
        "wb": w_proj_b.astype(BF16),
        "wo": w_out.astype(BF16),
        "wq_t": jnp.swapaxes(w_query, 1, 2).astype(BF16),
        "keys": sub_keys.astype(BF16),
        "u": expert_u.astype(BF16),
        "vt": jnp.swapaxes(expert_v, 1, 2).astype(BF16),
    }
    return w


def _trunk(x, c, w_mod, b_mod, norm1_gain, norm2_gain, sgu_gain, conv_w, conv_b, ssd_gain, final_gain, w):
    bsz, seq, _ = x.shape
    t = bsz * seq
    mod_all = _modulation(c, w_mod, b_mod)
    xf = x.reshape(t, D_MODEL)
    hn = _prenorm(x, norm1_gain[0], mod_all[0], 0, 1).reshape(t, D_MODEL)
    for l in range(DEPTH):
        mod = mod_all[l]
        uv =_matmul_act(hn, w["w_uv"][l], _gelu, BF16, 512, "proj_uv")
        z_act = _matmul_act(hn, w["w_z"][l], _silu, BF16, 512, "proj_z")
        xbc = _matmul_act(hn, w["w_xbc"][l], _identity, F32, 256, "proj_xbc")
        dt_raw = _matmul_act(hn, w["w_dt"][l], _identity, F32, 512, "proj_dt")
        gates = _matmul_act(hn, w["w_gate"][l], _sigmoid, BF16, 512, "proj_gate")
        pa = _sgu(uv, gates, sgu_gain[l], w["ws"][l], w["bs"][l], w["wa"][l])
        xbc_act = _conv_silu(xbc.reshape(bsz, seq, SSD_CONV_DIM), conv_w[l], conv_b[l])
        dt3 = dt_raw.reshape(bsz, seq, LANES)
        yf = _ssd_scan(xbc_act, dt3, w["dt_bias"][l], w["a_log"][l], reverse=False)
        yb = _ssd_scan(xbc_act, dt3, w["dt_bias"][l], w["a_log"][l], reverse=True)
        xf, h2t = _tail(yf.reshape(t, SSD_INNER), yb.reshape(t, SSD_INNER),
                       xbc_act.reshape(t, SSD_CONV_DIM), z_act, gates, pa, xf, mod,
                       w["d_skip"][l], ssd_gain[l], w["wb"][l], w["wo"][l], norm2_gain[l], seq)
        c0, a, r1, b = _route(h2t, w["wq_t"][l], w["keys"][l])
        if l == DEPTH - 1:
            xf = _peer(h2t, w["u"][l], w["vt"][l], c0, a, r1, b, xf, mod, final_gain, mod, seq, final_norm=True)
        else:
            xf, hn = _peer(h2t, w["u"][l], w["vt"][l], c0, a, r1, b, xf, mod, norm1_gain[l + 1],
                           mod_all[l + 1], seq, final_norm=False)
    return xf.reshape(bsz, seq, D_MODEL)


def kernel(x_prompt, x_sample, c_prompt, c_sample, w_mod, b_mod, norm1_gain, norm2_gain, w_in, sgu_gain, w_spatial, b_spatial, conv_w, conv_b, dt_bias, a_log, d_skip, ssd_gain, w_proj_a, w_proj_b, w_out, w_query, sub_keys, expert_u, expert_v, final_gain):
    w = _prepare_weights(w_in, w_spatial, b_spatial, dt_bias, a_log, d_skip, w_proj_a, w_proj_b,
                         w_out, w_query, sub_keys, expert_u, expert_v)
    args = (w_mod, b_mod, norm1_gain, norm2_gain, sgu_gain, conv_w, conv_b, ssd_gain, final_gain, w)
    y_prompt = _trunk(x_prompt, c_prompt, *args)
    y_sample = _trunk(x_sample, c_sample, *args)
    return (y_prompt, y_sample)
```

```python
import functools

import jax
import jax.numpy as jnp
from jax import lax
from jax.experimental import pallas as pl
from jax.experimental.pallas import tpu as pltpu

F32 = jnp.float32
BF16 = jnp.bfloat16

D_MODEL = 1024
DEPTH = 4
CHUNK = 128
SGU_WIDTH = 1024
SGU_GROUPS = 8
SSD_INNER = 2048
SSD_HEAD_DIM = 64
SSD_HEADS = 32
SSD_GROUPS = 4
SSD_STATE = 128
SSD_CONV = 5
SSD_CONV_DIM = 3072
PEER_HEADS = 8
PEER_NKEYS = 128
PEER_EXPERTS = PEER_NKEYS * PEER_NKEYS
PEER_HALF = 128
PEER_TOPK = 16
EPS = 1e-6

LANES = 128
MIB = 1024 * 1024
NEG_BIG = -3.0e38

OFF_UV, OFF_Z, OFF_XBC, OFF_DT, OFF_GATE, OFF_END = 0, 2048, 4096, 7168, 7232, 9280


def _params(semantics, vmem_mib=48):
    return pltpu.CompilerParams(dimension_semantics=semantics, vmem_limit_bytes=vmem_mib * MIB)


def _sigmoid(x):
    return 1.0 / (1.0 + jnp.exp(-x))


def _silu(x):
    return x * _sigmoid(x)


def _gelu(x):
    return 0.5 * x * (1.0 + jnp.tanh(0.7978845608028654 * (x + 0.044715 * (x * x * x))))


def _gelu_sigmoid_form(x):
    t = (x * x) * (-2.0 * 0.7978845608028654 * 0.044715) + (-2.0 * 0.7978845608028654)
    return x / (1.0 + jnp.exp(x * t))


def _bf16_pair_words(x):
    u = pltpu.bitcast(x.astype(BF16).astype(F32), jnp.uint32)
    return u | (u >> 16)


def _row_as_bf16_tile(words):
    return pltpu.bitcast(jnp.broadcast_to(words, (8, LANES)), BF16)


def _softplus(x):
    return jnp.maximum(x, 0.0) + jnp.log(1.0 + jnp.exp(-jnp.abs(x)))


def _mod_kernel(c_ref, w_ref, b_ref, o_ref):
    c = c_ref[...]
    o_ref[0] = jnp.dot(_silu(c), w_ref[0], preferred_element_type=F32,
                       precision=lax.Precision.HIGHEST) + b_ref[0]


def _modulation(c, w_mod, b_mod):
    bsz = c.shape[0]
    bp = -(-bsz // 8) * 8
    cp = jnp.pad(c, ((0, bp - bsz), (0, 0)))
    out = pl.pallas_call(
        _mod_kernel,
        out_shape=jax.ShapeDtypeStruct((DEPTH, bp, 6 * D_MODEL), F32),
        grid=(DEPTH, 6),
        in_specs=[pl.BlockSpec((bp, D_MODEL), lambda l, j: (0, 0)),
                  pl.BlockSpec((1, D_MODEL, D_MODEL), lambda l, j: (l, 0, j)),
                  pl.BlockSpec((1, 1, D_MODEL), lambda l, j: (l, 0, j))],
        out_specs=pl.BlockSpec((1, bp, D_MODEL), lambda l, j: (l, 0, j)),
        compiler_params=_params(("parallel", "parallel")),
        name="modulation",
    )(cp, w_mod, b_mod.reshape(DEPTH, 1, 6 * D_MODEL))
    return out[:, :bsz].reshape(DEPTH, bsz, 6, 1, D_MODEL)


def _prenorm_kernel(x_ref, gain_ref, sh_ref, sc_ref, o_ref):
    x = x_ref[0]
    y = x * lax.rsqrt(jnp.mean(x * x, axis=-1, keepdims=True) + EPS) * gain_ref[...]
    o_ref[0] = (y * (1.0 + sc_ref[...]) + sh_ref[...]).astype(o_ref.dtype)


def _prenorm(x, gain, mod, shift_idx, scale_idx, tl=512):
    bsz, seq, _ = x.shape
    tl = min(tl, seq)
    return pl.pallas_call(
        _prenorm_kernel,
        out_shape=jax.ShapeDtypeStruct((bsz, seq, D_MODEL), BF16),
        grid=(bsz, seq // tl),
        in_specs=[pl.BlockSpec((1, tl, D_MODEL), lambda b, i: (b, i, 0)),
                  pl.BlockSpec((1, D_MODEL), lambda b, i: (0, 0)),
                  pl.BlockSpec((None, None, 1, D_MODEL), lambda b, i: (b, shift_idx, 0, 0)),
                  pl.BlockSpec((None, None, 1, D_MODEL), lambda b, i: (b, scale_idx, 0, 0))],
        out_specs=pl.BlockSpec((1, tl, D_MODEL), lambda b, i: (b, i, 0)),
        compiler_params=_params(("parallel", "parallel")),
        name="prenorm",
    )(x, gain.reshape(1, D_MODEL), mod, mod)


def _matmul_kernel(a_ref, w_ref, o_ref, *, act):
    acc = jnp.dot(a_ref[...], w_ref[...], preferred_element_type=F32)
    o_ref[...] = act(acc).astype(o_ref.dtype)


def _matmul_act(a, w, act, out_dtype, tm, name):
    m, k = a.shape
    n = w.shape[1]
    tm = min(tm, m)
    return pl.pallas_call(
        functools.partial(_matmul_kernel, act=act),
        out_shape=jax.ShapeDtypeStruct((m, n), out_dtype),
        grid=(m // tm,),
        in_specs=[pl.BlockSpec((tm, k), lambda i: (i, 0)),
                  pl.BlockSpec((k, n), lambda i: (0, 0))],
        out_specs=pl.BlockSpec((tm, n), lambda i: (i, 0)),
        compiler_params=_params(("parallel",)),
        name=name,
    )(a, w)


def _sgu_kernel(uv_ref, ga_ref, gain_ref, ws_ref, bs_ref, wa_ref, o_ref, ya_ref):
    rows = uv_ref.shape[0]
    v = uv_ref[:, SGU_WIDTH:].astype(F32)
    vn = (v * lax.rsqrt(jnp.mean(v * v, axis=-1, keepdims=True) + EPS) * gain_ref[...]).astype(BF16)
    gdim = SGU_WIDTH // SGU_GROUPS
    for n in range(rows // CHUNK):
        r0 = n * CHUNK
        for g in range(SGU_GROUPS):
            c0 = g * gdim
            mixed = jnp.dot(ws_ref[g], vn[r0:r0 + CHUNK, c0:c0 + gdim],
                            preferred_element_type=F32) + bs_ref[g]
            u = uv_ref[r0:r0 + CHUNK, c0:c0 + gdim].astype(F32)
            ya_ref[r0:r0 + CHUNK, c0:c0 + gdim] = (u * mixed).astype(BF16)
    pa = jnp.dot(ya_ref[...], wa_ref[...], preferred_element_type=F32)
    o_ref[...] = (ga_ref[...].astype(F32) * pa).astype(o_ref.dtype)


def _sgu(uv, gates, sgu_gain, ws, bs_full, wa, ts=256):
    t = uv.shape[0]
    return pl.pallas_call(
        _sgu_kernel,
        out_shape=jax.ShapeDtypeStruct((t, D_MODEL), BF16),
        grid=(t // ts,),
        in_specs=[pl.BlockSpec((ts, 2 * SGU_WIDTH), lambda i: (i, 0)),
                  pl.BlockSpec((ts, D_MODEL), lambda i: (i, 0)),
                  pl.BlockSpec((1, SGU_WIDTH), lambda i: (0, 0)),
                  pl.BlockSpec((SGU_GROUPS, CHUNK, CHUNK), lambda i: (0, 0, 0)),
                  pl.BlockSpec((SGU_GROUPS, CHUNK, CHUNK), lambda i: (0, 0, 0)),
                  pl.BlockSpec((SGU_WIDTH, D_MODEL), lambda i: (0, 0))],
        out_specs=pl.BlockSpec((ts, D_MODEL), lambda i: (i, 0)),
        scratch_shapes=[pltpu.VMEM((ts, SGU_WIDTH), BF16)],
        compiler_params=_params(("parallel",)),
        name="sgu_proj_a",
    )(uv, gates, sgu_gain.reshape(1, SGU_WIDTH), ws, bs_full, wa)


HALO = 16
CONV_COLS = 512


def _xbc_conv_kernel(h_ref, prev_ref, next_ref, w_ref, wdt_ref, cw_ref, cb_ref, o_ref, dt_ref, ext_ref, ext2_ref):
    i = pl.program_id(1)
    last = pl.num_programs(1) - 1
    tl = h_ref.shape[1]
    rows = jnp.concatenate([prev_ref[0], h_ref[0], next_ref[0]], axis=0)
    ext_refs = (ext_ref, ext2_ref)
    width = ext_ref.shape[1]
    for c in range(w_ref.shape[1] // width):
        cols = slice(c * width, (c + 1) * width)
        ext = ext_refs[c % 2]
        ext[...] = jnp.dot(rows, w_ref[:, cols], preferred_element_type=F32)
        ext[0:HALO, :] = jnp.where(i > 0, ext[0:HALO, :], 0.0)
        ext[HALO + tl:, :] = jnp.where(i < last, ext[HALO + tl:, :], 0.0)
        acc = jnp.zeros((tl, width), F32) + cb_ref[:, cols]
        for k in range(SSD_CONV):
            start = HALO - SSD_CONV // 2 + k
            acc = acc + cw_ref[k:k + 1, cols] * ext[start:start + tl, :]
        o_ref[0, :, cols] = _silu(acc).astype(o_ref.dtype)
    dt_ref[0] = jnp.dot(h_ref[0], wdt_ref[...], preferred_element_type=F32)


def _proj_xbc_conv(hn, w_xbc, w_dt, conv_w, conv_b, tl=256):
    bsz, seq, _ = hn.shape
    ch = w_xbc.shape[1]
    tl = min(tl, seq)
    nblk = tl // HALO
    last_blk = seq // HALO - 1
    const = lambda b, i: (0, 0)
    return pl.pallas_call(
        _xbc_conv_kernel,
        out_shape=(jax.ShapeDtypeStruct((bsz, seq, ch), BF16), jax.ShapeDtypeStruct((bsz, seq, LANES), F32)),
        grid=(bsz, seq // tl),
        in_specs=[pl.BlockSpec((1, tl, D_MODEL), lambda b, i: (b, i, 0)),
                  pl.BlockSpec((1, HALO, D_MODEL), lambda b, i: (b, jnp.maximum(i * nblk - 1, 0), 0)),
                  pl.BlockSpec((1, HALO, D_MODEL), lambda b, i: (b, jnp.minimum((i + 1) * nblk, last_blk), 0)),
                  pl.BlockSpec((D_MODEL, ch), const),
                  pl.BlockSpec((D_MODEL, LANES), const),
                  pl.BlockSpec((SSD_CONV, ch), const),
                  pl.BlockSpec((1, ch), const)],
        out_specs=(pl.BlockSpec((1, tl, ch), lambda b, i: (b, i, 0)),
                   pl.BlockSpec((1, tl, LANES), lambda b, i: (b, i, 0))),
        scratch_shapes=[pltpu.VMEM((tl + 2 * HALO, CONV_COLS), F32), pltpu.VMEM((tl + 2 * HALO, CONV_COLS), F32)],
        compiler_params=_params(("parallel", "parallel")),
        name="proj_xbc_conv",
    )(hn, hn, hn, w_xbc, w_dt, conv_w, conv_b.reshape(1, ch))


def _ssd_kernel(xbc_ref, dt_ref, bias_ref, alog_ref, o_ref, state_ref, *, reverse):
    c = pl.program_id(1)

    @pl.when(c == 0)
    def _():
        state_ref[...] = jnp.zeros_like(state_ref)

    col0 = SSD_HEADS if reverse else 0
    row = lax.broadcasted_iota(jnp.int32, (CHUNK, CHUNK), 0)
    lane = lax.broadcasted_iota(jnp.int32, (CHUNK, CHUNK), 1)
    tri = (row <= lane) if reverse else (row >= lane)
    lo_half = lane < SSD_HEAD_DIM

    dt = _softplus(dt_ref[0] + bias_ref[...])
    adt = dt * (-jnp.exp(alog_ref[...]))
    cs = adt
    shift = 1
    while shift < CHUNK:
        if reverse:
            moved = pltpu.roll(cs, CHUNK - shift, axis=0)
            cs = cs + jnp.where(row < CHUNK - shift, moved, 0.0)
        else:
            moved = pltpu.roll(cs, shift, axis=0)
            cs = cs + jnp.where(row >= shift, moved, 0.0)
        shift *= 2
    cs_t = cs.T
    dt_t = dt.T
    end = 0 if reverse else CHUNK - 1
    cs_end = jnp.broadcast_to(cs_t[:, end:end + 1], (CHUNK, CHUNK))
    w_all = dt_t * jnp.exp(cs_end - cs_t)
    dec_all = jnp.exp(cs_end)

    hpg = SSD_HEADS // SSD_GROUPS
    for g in range(SSD_GROUPS):
        b_off = SSD_INNER + g * SSD_STATE
        c_off = SSD_INNER + SSD_GROUPS * SSD_STATE + g * SSD_STATE
        bg = xbc_ref[0, :, b_off:b_off + SSD_STATE]
        cg = xbc_ref[0, :, c_off:c_off + SSD_STATE]
        cb = lax.dot_general(cg, bg, (((1,), (1,)), ((), ())), preferred_element_type=F32)
        cg32 = cg.astype(F32)
        bg_t = bg.astype(F32).T
        for j in range(hpg // 2):
            pair = g * (hpg // 2) + j
            lhs_parts, lhs2_parts = [], []
            for k in range(2):
                col = col0 + 2 * pair + k
                lmat = jnp.broadcast_to(cs[:, col:col + 1], (CHUNK, CHUNK))
                decay = jnp.where(tri, jnp.exp(lmat - cs_t[col:col + 1, :]), 0.0)
                lhs_parts.append((cb * decay * dt_t[col:col + 1, :]).astype(BF16))
                lhs_parts.append((cg32 * jnp.exp(lmat)).astype(BF16))
                lhs2_parts.append((bg_t * w_all[col:col + 1, :]).astype(BF16))
            xs = xbc_ref[0, :, pair * LANES:(pair + 1) * LANES]
            zero = jnp.zeros_like(xs)
            x0 = jnp.where(lo_half, xs, zero)
            x1 = jnp.where(lo_half, zero, xs)
            st = state_ref[pair]
            st16 = st.astype(BF16)
            s0 = jnp.where(lo_half, st16, zero)
            s1 = jnp.where(lo_half, zero, st16)
            lhs = jnp.concatenate(lhs_parts, axis=1)
            rhs = jnp.concatenate([x0, s0, x1, s1], axis=0)
            o_ref[0, :, pair * LANES:(pair + 1) * LANES] = jnp.dot(
                lhs, rhs, preferred_element_type=F32).astype(o_ref.dtype)
            lhs2 = jnp.concatenate(lhs2_parts, axis=1)
            rhs2 = jnp.concatenate([x0, x1], axis=0)
            col_a = col0 + 2 * pair
            dec = jnp.where(lo_half, dec_all[col_a:col_a + 1, :], dec_all[col_a + 1:col_a + 2, :])
            state_ref[pair] = st * dec + jnp.dot(lhs2, rhs2, preferred_element_type=F32)


def _ssd_scan(xbc_act, dt_raw, dt_bias_row, alog_row, reverse):
    bsz, seq, _ = xbc_act.shape
    nc = seq // CHUNK
    if reverse:
        cmap = lambda b, c: (b, nc - 1 - c, 0)
    else:
        cmap = lambda b, c: (b, c, 0)
    return pl.pallas_call(
        functools.partial(_ssd_kernel, reverse=reverse),
        out_shape=jax.ShapeDtypeStruct((bsz, seq, SSD_INNER), BF16),
        grid=(bsz, nc),
        in_specs=[pl.BlockSpec((1, CHUNK, SSD_CONV_DIM), cmap),
                  pl.BlockSpec((1, CHUNK, LANES), cmap),
                  pl.BlockSpec((1, LANES), lambda b, c: (0, 0)),
                  pl.BlockSpec((1, LANES), lambda b, c: (0, 0))],
        out_specs=pl.BlockSpec((1, CHUNK, SSD_INNER), cmap),
        scratch_shapes=[pltpu.VMEM((SSD_HEADS // 2, SSD_STATE, LANES), F32)],
        compiler_params=_params(("parallel", "arbitrary")),
        name="ssd_bwd" if reverse else "ssd_fwd",
    )(xbc_act, dt_raw, dt_bias_row, alog_row)


def _tail_kernel(yf_ref, yb_ref, xs_ref, z_ref, gb_ref, pa_ref, x_ref, g1_ref, dskip_ref, gain_ref,
                 wb_ref, wo_ref, n2_ref, sh2_ref, sc2_ref, xo_ref, h2t_ref):
    y = yf_ref[...].astype(F32) + yb_ref[...].astype(F32) + dskip_ref[...] * xs_ref[...].astype(F32)
    y = y * z_ref[...].astype(F32)
    gw = SSD_INNER // SSD_GROUPS
    parts = []
    for g in range(SSD_GROUPS):
        yg = y[:, g * gw:(g + 1) * gw]
        parts.append(yg * lax.rsqrt(jnp.mean(yg * yg, axis=-1, keepdims=True) + EPS))
    yn = (jnp.concatenate(parts, axis=1) * gain_ref[...]).astype(BF16)
    pb = jnp.dot(yn, wb_ref[...], preferred_element_type=F32)
    merged = pa_ref[...].astype(F32) + gb_ref[...].astype(F32) * pb
    out = jnp.dot(merged.astype(BF16), wo_ref[...], preferred_element_type=F32)
    x = x_ref[...] + g1_ref[...] * out
    xo_ref[...] = x
    h = x * lax.rsqrt(jnp.mean(x * x, axis=-1, keepdims=True) + EPS) * n2_ref[...]
    h2t_ref[...] = (h * (1.0 + sc2_ref[...]) + sh2_ref[...]).T.astype(h2t_ref.dtype)


def _tail(yf, yb, xbc_act, z_act, gates, pa, x, mod, dskip_row, ssd_gain, wb, wo, norm2_gain, seq, tm=256):
    t = x.shape[0]
    tm = min(tm, seq)
    per_row = seq // tm
    row = lambda i: (i, 0)
    const = lambda i: (0, 0)
    modspec = lambda k: pl.BlockSpec((None, None, 1, D_MODEL), lambda i: (i // per_row, k, 0, 0))
    return pl.pallas_call(
        _tail_kernel,
        out_shape=(jax.ShapeDtypeStruct((t, D_MODEL), F32), jax.ShapeDtypeStruct((D_MODEL, t), BF16)),
        grid=(t // tm,),
        in_specs=[pl.BlockSpec((tm, SSD_INNER), row),
                  pl.BlockSpec((tm, SSD_INNER), row),
                  pl.BlockSpec((tm, SSD_INNER), row),
                  pl.BlockSpec((tm, SSD_INNER), row),
                  pl.BlockSpec((tm, D_MODEL), lambda i: (i, 1)),
                  pl.BlockSpec((tm, D_MODEL), row),
                  pl.BlockSpec((tm, D_MODEL), row),
                  modspec(2),
                  pl.BlockSpec((1, SSD_INNER), const),
                  pl.BlockSpec((1, SSD_INNER), const),
                  pl.BlockSpec((SSD_INNER, D_MODEL), const),
                  pl.BlockSpec((D_MODEL, D_MODEL), const),
                  pl.BlockSpec((1, D_MODEL), const),
                  modspec(3),
                  modspec(4)],
        out_specs=(pl.BlockSpec((tm, D_MODEL), row), pl.BlockSpec((D_MODEL, tm), lambda i: (0, i))),
        compiler_params=_params(("parallel",)),
        name="mix_tail",
    )(yf, yb, xbc_act, z_act, gates, pa, x, mod, dskip_row, ssd_gain.reshape(1, SSD_INNER), wb, wo,
      norm2_gain.reshape(1, D_MODEL), mod, mod)


CAND_PAIRS = [(a, b) for a in range(PEER_TOPK) for b in range(PEER_TOPK) if (a + 1) * (b + 1) <= PEER_TOPK]
RANK_OUTSIDE = float(2 * PEER_TOPK)
SUBLANES = 8


def _sorted_desc(cols):
    c = list(cols)
    n = len(c)
    k = 2
    while k <= n:
        j = k // 2
        while j >= 1:
            for i in range(n):
                l = i ^ j
                if l > i:
                    hi, lo = jnp.maximum(c[i], c[l]), jnp.minimum(c[i], c[l])
                    c[i], c[l] = (hi, lo) if (i & k) == 0 else (lo, hi)
            j //= 2
        k *= 2
    return c


def _top_values(s):
    c = _sorted_desc([s[v * SUBLANES:(v + 1) * SUBLANES, :] for v in range(PEER_NKEYS // SUBLANES)])
    tops = []
    for r in range(PEER_TOPK):
        m = jnp.max(c[0], axis=0, keepdims=True)
        tops.append(m)
        depth = PEER_TOPK - 1 - r
        if depth:
            hit = c[0] == m
            c = [jnp.where(hit, c[q + 1], c[q]) for q in range(depth)]
    return tops


def _count_at_least(x, thresholds):
    cnt = jnp.zeros(x.shape, F32)
    for n, th in enumerate(thresholds):
        cnt = jnp.where(x >= th, float(n + 1), cnt)
    return cnt


def _route_kernel(ht_ref, wq_ref, keys_ref, c0_ref, a_ref, r1_ref, b_ref, top_ref, s_ref):
    q_t = jnp.dot(wq_ref[...], ht_ref[...], preferred_element_type=F32)
    for hd in range(PEER_HEADS):
        for half in range(2):
            r0 = (hd * 2 + half) * PEER_HALF
            q = q_t[r0:r0 + PEER_HALF, :].astype(BF16)
            s = jnp.dot(keys_ref[half], q, preferred_element_type=F32)
            s_ref[half, hd] = s
            for r, m in enumerate(_top_values(s)):
                top_ref[half, r, hd:hd + 1, :] = m
    cands = [top_ref[0, a] + top_ref[1, b] for a, b in CAND_PAIRS]
    best = cands[0]
    work = list(cands)
    kth = best
    for r in range(PEER_TOPK + 1):
        prev = kth
        kth = work[0]
        for x in work[1:]:
            kth = jnp.maximum(kth, x)
        if r < PEER_TOPK:
            work = [jnp.where(x == kth, NEG_BIG, x) for x in work]
    tau = 0.5 * (prev + kth)
    z = jnp.zeros_like(best)
    for x in cands:
        z = z + jnp.where(x >= tau, jnp.exp(x - best), 0.0)
    inv_z = 1.0 / z
    for hd in range(PEER_HEADS):
        s0 = s_ref[0, hd]
        s1 = s_ref[1, hd]
        tau_h = tau[hd:hd + 1, :]
        tops1 = [top_ref[1, b, hd:hd + 1, :] for b in range(PEER_TOPK)]
        floor0 = top_ref[0, PEER_TOPK - 1, hd:hd + 1, :]
        c0_ref[hd] = _bf16_pair_words(_count_at_least(s0, [jnp.maximum(tau_h - t, floor0) for t in tops1]))
        reached = _count_at_least(s1, tops1[::-1])
        r1_ref[hd] = jnp.where(reached > 0.0, float(PEER_TOPK) - reached, RANK_OUTSIDE).astype(BF16)
        a_ref[hd] = _bf16_pair_words(jnp.exp(s0 - top_ref[0, 0, hd:hd + 1, :]))
        b_ref[hd] = (jnp.exp(s1 - tops1[0]) * inv_z[hd:hd + 1, :]).astype(BF16)


def _route(h2t, wq_t, keys, tt=256):
    t = h2t.shape[1]
    shp32 = jax.ShapeDtypeStruct((PEER_HEADS, PEER_NKEYS, t), jnp.uint32)
    shp16 = jax.ShapeDtypeStruct((PEER_HEADS, PEER_NKEYS, t), BF16)
    ospec = pl.BlockSpec((PEER_HEADS, PEER_NKEYS, tt), lambda i: (0, 0, i))
    return pl.pallas_call(
        _route_kernel,
        out_shape=(shp32, shp32, shp16, shp16),
        grid=(t // tt,),
        in_specs=[pl.BlockSpec((D_MODEL, tt), lambda i: (0, i)),
                  pl.BlockSpec((2 * PEER_HEADS * PEER_HALF, D_MODEL), lambda i: (0, 0)),
                  pl.BlockSpec((2, PEER_NKEYS, PEER_HALF), lambda i: (0, 0, 0))],
        out_specs=(ospec, ospec, ospec, ospec),
        scratch_shapes=[pltpu.VMEM((2, PEER_TOPK, PEER_HEADS, tt), F32),
                        pltpu.VMEM((2, PEER_HEADS, PEER_NKEYS, tt), F32)],
        compiler_params=_params(("parallel",)),
        name="peer_route",
    )(h2t, wq_t, keys)


E_BLK = 2048
I_BLK = E_BLK // PEER_NKEYS
E_SUB = 256
I_SUB = E_SUB // PEER_NKEYS
ROWS16 = 16


def _peer_kernel(ht_ref, u_ref, vt_ref, c0_ref, a_ref, r1_ref, b_ref, x_ref, g2_ref, gain_ref, sh_ref, sc_ref,
                 *refs, final_norm):
    n_out = 1 if final_norm else 2
    o_ref = refs[0]
    act0_ref, act1_ref, p0_ref, p1_ref, acc_ref = refs[n_out:]
    e = pl.program_id(1)
    tt = ht_ref.shape[1]
    t_sub = act0_ref.shape[1]
    n_jp = PEER_NKEYS // ROWS16
    act_refs = (act0_ref, act1_ref)
    p_refs = (p0_ref, p1_ref)
    units = [(sb, th) for sb in range(E_BLK // E_SUB) for th in range(tt // t_sub)]

    @pl.when(e == 0)
    def _():
        acc_ref[...] = jnp.zeros_like(acc_ref)

    def activation_matmul(n):
        sb, th = units[n]
        act_refs[n % 2][...] = jnp.dot(u_ref[sb * E_SUB:(sb + 1) * E_SUB, :],
                                       ht_ref[:, th * t_sub:(th + 1) * t_sub],
                                       preferred_element_type=F32)

    def gated_activations(n):
        sb, th = units[n]
        act_ref, p_ref = act_refs[n % 2], p_refs[n % 2]
        ils = [sb * I_SUB + k for k in range(I_SUB)]
        for tc in range(t_sub // LANES):
            ls = slice(tc * LANES, (tc + 1) * LANES)
            ts = slice(th * t_sub + tc * LANES, th * t_sub + (tc + 1) * LANES)
            accs = [[None] * n_jp for _ in ils]
            for hd in range(PEER_HEADS):
                cnt = [_row_as_bf16_tile(c0_ref[hd, il:il + 1, ts]) for il in ils]
                wgt = [_row_as_bf16_tile(a_ref[hd, il:il + 1, ts]) for il in ils]
                for jp in range(n_jp):
                    js = slice(jp * ROWS16, (jp + 1) * ROWS16)
                    r1 = r1_ref[hd, js, ts]
                    bb = b_ref[hd, js, ts]
                    for k in range(I_SUB):
                        term = jnp.minimum(jnp.maximum(cnt[k] - r1, 0.0), wgt[k]) * bb
                        accs[k][jp] = term if accs[k][jp] is None else accs[k][jp] + term
            for k in range(I_SUB):
                for jp in range(n_jp):
                    r0 = k * PEER_NKEYS + jp * ROWS16
                    gel = _gelu_sigmoid_form(act_ref[r0:r0 + ROWS16, ls].astype(BF16))
                    p_ref[r0:r0 + ROWS16, ls] = accs[k][jp] * gel

    def retrieval_matmul(n):
        sb, th = units[n]
        cols = slice(th * t_sub, (th + 1) * t_sub)
        acc_ref[:, cols] += jnp.dot(vt_ref[:, sb * E_SUB:(sb + 1) * E_SUB], p_refs[n % 2][...],
                                    preferred_element_type=F32)

    activation_matmul(0)
    for n in range(len(units)):
        if n + 1 < len(units):
            activation_matmul(n + 1)
        gated_activations(n)
        retrieval_matmul(n)

    @pl.when(e == pl.num_programs(1) - 1)
    def _():
        x = x_ref[...] + g2_ref[...] * acc_ref[...].T
        y = x * lax.rsqrt(jnp.mean(x * x, axis=-1, keepdims=True) + EPS) * gain_ref[...]
        if final_norm:
            o_ref[...] = y
        else:
            o_ref[...] = x
            refs[1][...] = (y * (1.0 + sc_ref[...]) + sh_ref[...]).astype(BF16)


def _peer(h2t, u16, vt16, c0, a, r1, b, x, mod, gain, mod_next, seq, final_norm, tt=512):
    t = h2t.shape[1]
    tt = min(tt, seq)
    per_row = seq // tt
    ne = PEER_EXPERTS // E_BLK
    tok = lambda i, e: (i, 0)
    blk_i = pl.BlockSpec((PEER_HEADS, I_BLK, tt), lambda i, e: (0, e, i))
    full_j = pl.BlockSpec((PEER_HEADS, PEER_NKEYS, tt), lambda i, e: (0, 0, i))
    modspec = lambda tbl_idx: pl.BlockSpec((None, None, 1, D_MODEL), lambda i, e: (i // per_row, tbl_idx, 0, 0))
    x_shape = jax.ShapeDtypeStruct((t, D_MODEL), F32)
    x_spec = pl.BlockSpec((tt, D_MODEL), tok)
    return pl.pallas_call(
        functools.partial(_peer_kernel, final_norm=final_norm),
        out_shape=x_shape if final_norm else (x_shape, jax.ShapeDtypeStruct((t, D_MODEL), BF16)),
        grid=(t // tt, ne),
        in_specs=[pl.BlockSpec((D_MODEL, tt), lambda i, e: (0, i)),
                  pl.BlockSpec((E_BLK, D_MODEL), lambda i, e: (e, 0)),
                  pl.BlockSpec((D_MODEL, E_BLK), lambda i, e: (0, e)),
                  blk_i, blk_i, full_j, full_j,
                  x_spec,
                  modspec(5),
                  pl.BlockSpec((1, D_MODEL), lambda i, e: (0, 0)),
                  modspec(0),
                  modspec(1)],
        out_specs=x_spec if final_norm else (x_spec, x_spec),
        scratch_shapes=[pltpu.VMEM((E_SUB, tt), F32), pltpu.VMEM((E_SUB, tt), F32),
                        pltpu.VMEM((E_SUB, tt), BF16), pltpu.VMEM((E_SUB, tt), BF16),
                        pltpu.VMEM((D_MODEL, tt), F32)],
        compiler_params=_params(("parallel", "arbitrary")),
        name="peer_dense",
    )(h2t, u16, vt16, c0, a, r1, b, x, mod, gain.reshape(1, D_MODEL), mod_next, mod_next)


def _prepare_weights(w_in, w_spatial, b_spatial, dt_bias, a_log, d_skip, w_proj_a, w_proj_b, w_out,
                     w_query, sub_keys, expert_u, expert_v):
    pad = LANES - 2 * SSD_HEADS
    w = {
        "w_uv": w_in[:, :, OFF_UV:OFF_Z].astype(BF16),
        "w_z": w_in[:, :, OFF_Z:OFF_XBC].astype(BF16),
        "w_xbc": w_in[:, :, OFF_XBC:OFF_DT].astype(BF16),
        "w_dt": jnp.pad(w_in[:, :, OFF_DT:OFF_GATE], ((0, 0), (0, 0), (0, pad))).astype(BF16),
        "w_gate": w_in[:, :, OFF_GATE:OFF_END].astype(BF16),
        "ws": w_spatial.astype(BF16),
        "bs": jnp.broadcast_to(b_spatial[..., None], b_spatial.shape + (CHUNK,)),
        "dt_bias": jnp.pad(dt_bias.reshape(DEPTH, 1, 2 * SSD_HEADS), ((0, 0), (0, 0), (0, pad))),
        "a_log": jnp.pad(a_log.reshape(DEPTH, 1, 2 * SSD_HEADS), ((0, 0), (0, 0), (0, pad))),
        "d_skip": jnp.repeat(d_skip, SSD_HEAD_DIM, axis=1).reshape(DEPTH, 1, SSD_INNER),
        "wa": w_proj_a.astype(BF16),
        "wb": w_proj_b.astype(BF16),
        "wo": w_out.astype(BF16),
        "wq_t": jnp.swapaxes(w_query, 1, 2).astype(BF16),
        "keys": sub_keys.astype(BF16),
        "u": expert_u.astype(BF16),
        "vt": jnp.swapaxes(expert_v, 1, 2).astype(BF16),
    }
    return w


def _trunk(x, c, w_mod, b_mod, norm1_gain, norm2_gain, sgu_gain, conv_w, conv_b, ssd_gain, final_gain, w):
    bsz, seq, _ = x.shape
    t = bsz * seq
    mod_all = _modulation(c, w_mod, b_mod)
    xf = x.reshape(t, D_MODEL)
    hn = _prenorm(x, norm1_gain[0], mod_all[0], 0, 1).reshape(t, D_MODEL)
    for l in range(DEPTH):
        mod = mod_all[l]
        uv =_matmul_act(hn, w["w_uv"][l], _gelu, BF16, 512, "proj_uv")
        z_act = _matmul_act(hn, w["w_z"][l], _silu, BF16, 512, "proj_z")
        gates = _matmul_act(hn, w["w_gate"][l], _sigmoid, BF16, 512, "proj_gate")
        pa = _sgu(uv, gates, sgu_gain[l], w["ws"][l], w["bs"][l], w["wa"][l])
        xbc_act, dt3 = _proj_xbc_conv(hn.reshape(bsz, seq, D_MODEL), w["w_xbc"][l], w["w_dt"][l],
                                      conv_w[l], conv_b[l])
        yf = _ssd_scan(xbc_act, dt3, w["dt_bias"][l], w["a_log"][l], reverse=False)
        yb = _ssd_scan(xbc_act, dt3, w["dt_bias"][l], w["a_log"][l], reverse=True)
        xf, h2t = _tail(yf.reshape(t, SSD_INNER), yb.reshape(t, SSD_INNER),
                       xbc_act.reshape(t, SSD_CONV_DIM), z_act, gates, pa, xf, mod,
                       w["d_skip"][l], ssd_gain[l], w["wb"][l], w["wo"][l], norm2_gain[l], seq)
        c0, a, r1, b = _route(h2t, w["wq_t"][l], w["keys"][l])
        if l == DEPTH - 1:
            xf = _peer(h2t, w["u"][l], w["vt"][l], c0, a, r1, b, xf, mod, final_gain, mod, seq, final_norm=True)
        else:
            xf, hn = _peer(h2t, w["u"][l], w["vt"][l], c0, a, r1, b, xf, mod, norm1_gain[l + 1],
                           mod_all[l + 1], seq, final_norm=False)
    return xf.reshape(bsz, seq, D_MODEL)


def kernel(x_prompt, x_sample, c_prompt, c_sample, w_mod, b_mod, norm1_gain, norm2_gain, w_in, sgu_gain, w_spatial, b_spatial, conv_w, conv_b, dt_bias, a_log, d_skip, ssd_gain, w_proj_a, w_proj_b, w_out, w_query, sub_keys, expert_u, expert_v, final_gain):
    w = _prepare_weights(w_in, w_spatial, b_spatial, dt_bias, a_log, d_skip, w_proj_a, w_proj_b,
                         w_out, w_query, sub_keys, expert_u, expert_v)
    args = (w_mod, b_mod, norm1_gain, norm2_gain, sgu_gain, conv_w, conv_b, ssd_gain, final_gain, w)
    y_prompt = _trunk(x_prompt, c_prompt, *args)
    y_sample = _trunk(x_sample, c_sample, *args)
    return (y_prompt, y_sample)
```

```python
import functools

import jax
import jax.numpy as jnp
from jax import lax
from jax.experimental import pallas as pl
from jax.experimental.pallas import tpu as pltpu

F32 = jnp.float32
BF16 = jnp.bfloat16

D_MODEL = 1024
DEPTH = 4
CHUNK = 128
SGU_WIDTH = 1024
SGU_GROUPS = 8
SSD_INNER = 2048
SSD_HEAD_DIM = 64
SSD_HEADS = 32
SSD_GROUPS = 4
SSD_STATE = 128
SSD_CONV = 5
SSD_CONV_DIM = 3072
PEER_HEADS = 8
PEER_NKEYS = 128
PEER_EXPERTS = PEER_NKEYS * PEER_NKEYS
PEER_HALF = 128
PEER_TOPK = 16
EPS = 1e-6

LANES = 128
MIB = 1024 * 1024
NEG_BIG = -3.0e38

OFF_UV, OFF_Z, OFF_XBC, OFF_DT, OFF_GATE, OFF_END = 0, 2048, 4096, 7168, 7232, 9280


def _params(semantics, vmem_mib=48):
    return pltpu.CompilerParams(dimension_semantics=semantics, vmem_limit_bytes=vmem_mib * MIB)


def _sigmoid(x):
    return 1.0 / (1.0 + jnp.exp(-x))


def _silu(x):
    return x * _sigmoid(x)


def _gelu(x):
    return 0.5 * x * (1.0 + jnp.tanh(0.7978845608028654 * (x + 0.044715 * (x * x * x))))


def _gelu_sigmoid_form(x):
    t = (x * x) * (-2.0 * 0.7978845608028654 * 0.044715) + (-2.0 * 0.7978845608028654)
    return x / (1.0 + jnp.exp(x * t))


def _bf16_pair_words(x):
    u = pltpu.bitcast(x.astype(BF16).astype(F32), jnp.uint32)
    return u | (u >> 16)


def _row_as_bf16_tile(words):
    return pltpu.bitcast(jnp.broadcast_to(words, (8, LANES)), BF16)


def _softplus(x):
    return jnp.maximum(x, 0.0) + jnp.log(1.0 + jnp.exp(-jnp.abs(x)))


def _mod_kernel(c_ref, w_ref, b_ref, o_ref):
    c = c_ref[...]
    o_ref[0] = jnp.dot(_silu(c), w_ref[0], preferred_element_type=F32,
                       precision=lax.Precision.HIGHEST) + b_ref[0]


def _modulation(c, w_mod, b_mod):
    bsz = c.shape[0]
    bp = -(-bsz // 8) * 8
    cp = jnp.pad(c, ((0, bp - bsz), (0, 0)))
    out = pl.pallas_call(
        _mod_kernel,
        out_shape=jax.ShapeDtypeStruct((DEPTH, bp, 6 * D_MODEL), F32),
        grid=(DEPTH, 6),
        in_specs=[pl.BlockSpec((bp, D_MODEL), lambda l, j: (0, 0)),
                  pl.BlockSpec((1, D_MODEL, D_MODEL), lambda l, j: (l, 0, j)),
                  pl.BlockSpec((1, 1, D_MODEL), lambda l, j: (l, 0, j))],
        out_specs=pl.BlockSpec((1, bp, D_MODEL), lambda l, j: (l, 0, j)),
        compiler_params=_params(("parallel", "parallel")),
        name="modulation",
    )(cp, w_mod, b_mod.reshape(DEPTH, 1, 6 * D_MODEL))
    return out[:, :bsz].reshape(DEPTH, bsz, 6, 1, D_MODEL)


def _prenorm_kernel(x_ref, gain_ref, sh_ref, sc_ref, o_ref):
    x = x_ref[0]
    y = x * lax.rsqrt(jnp.mean(x * x, axis=-1, keepdims=True) + EPS) * gain_ref[...]
    o_ref[0] = (y * (1.0 + sc_ref[...]) + sh_ref[...]).astype(o_ref.dtype)


def _prenorm(x, gain, mod, shift_idx, scale_idx, tl=512):
    bsz, seq, _ = x.shape
    tl = min(tl, seq)
    return pl.pallas_call(
        _prenorm_kernel,
        out_shape=jax.ShapeDtypeStruct((bsz, seq, D_MODEL), BF16),
        grid=(bsz, seq // tl),
        in_specs=[pl.BlockSpec((1, tl, D_MODEL), lambda b, i: (b, i, 0)),
                  pl.BlockSpec((1, D_MODEL), lambda b, i: (0, 0)),
                  pl.BlockSpec((None, None, 1, D_MODEL), lambda b, i: (b, shift_idx, 0, 0)),
                  pl.BlockSpec((None, None, 1, D_MODEL), lambda b, i: (b, scale_idx, 0, 0))],
        out_specs=pl.BlockSpec((1, tl, D_MODEL), lambda b, i: (b, i, 0)),
        compiler_params=_params(("parallel", "parallel")),
        name="prenorm",
    )(x, gain.reshape(1, D_MODEL), mod, mod)


def _matmul_kernel(a_ref, w_ref, o_ref, *, act):
    acc = jnp.dot(a_ref[...], w_ref[...], preferred_element_type=F32)
    o_ref[...] = act(acc).astype(o_ref.dtype)


def _matmul_act(a, w, act, out_dtype, tm, name):
    m, k = a.shape
    n = w.shape[1]
    tm = min(tm, m)
    return pl.pallas_call(
        functools.partial(_matmul_kernel, act=act),
        out_shape=jax.ShapeDtypeStruct((m, n), out_dtype),
        grid=(m // tm,),
        in_specs=[pl.BlockSpec((tm, k), lambda i: (i, 0)),
                  pl.BlockSpec((k, n), lambda i: (0, 0))],
        out_specs=pl.BlockSpec((tm, n), lambda i: (i, 0)),
        compiler_params=_params(("parallel",)),
        name=name,
    )(a, w)


def _sgu_kernel(uv_ref, ga_ref, gain_ref, ws_ref, bs_ref, wa_ref, o_ref, ya_ref):
    rows = uv_ref.shape[0]
    v = uv_ref[:, SGU_WIDTH:].astype(F32)
    vn = (v * lax.rsqrt(jnp.mean(v * v, axis=-1, keepdims=True) + EPS) * gain_ref[...]).astype(BF16)
    gdim = SGU_WIDTH // SGU_GROUPS
    for n in range(rows // CHUNK):
        r0 = n * CHUNK
        for g in range(SGU_GROUPS):
            c0 = g * gdim
            mixed = jnp.dot(ws_ref[g], vn[r0:r0 + CHUNK, c0:c0 + gdim],
                            preferred_element_type=F32) + bs_ref[g]
            u = uv_ref[r0:r0 + CHUNK, c0:c0 + gdim].astype(F32)
            ya_ref[r0:r0 + CHUNK, c0:c0 + gdim] = (u * mixed).astype(BF16)
    pa = jnp.dot(ya_ref[...], wa_ref[...], preferred_element_type=F32)
    o_ref[...] = (ga_ref[...].astype(F32) * pa).astype(o_ref.dtype)


def _sgu(uv, gates, sgu_gain, ws, bs_full, wa, ts=256):
    t = uv.shape[0]
    return pl.pallas_call(
        _sgu_kernel,
        out_shape=jax.ShapeDtypeStruct((t, D_MODEL), BF16),
        grid=(t // ts,),
        in_specs=[pl.BlockSpec((ts, 2 * SGU_WIDTH), lambda i: (i, 0)),
                  pl.BlockSpec((ts, D_MODEL), lambda i: (i, 0)),
                  pl.BlockSpec((1, SGU_WIDTH), lambda i: (0, 0)),
                  pl.BlockSpec((SGU_GROUPS, CHUNK, CHUNK), lambda i: (0, 0, 0)),
                  pl.BlockSpec((SGU_GROUPS, CHUNK, CHUNK), lambda i: (0, 0, 0)),
                  pl.BlockSpec((SGU_WIDTH, D_MODEL), lambda i: (0, 0))],
        out_specs=pl.BlockSpec((ts, D_MODEL), lambda i: (i, 0)),
        scratch_shapes=[pltpu.VMEM((ts, SGU_WIDTH), BF16)],
        compiler_params=_params(("parallel",)),
        name="sgu_proj_a",
    )(uv, gates, sgu_gain.reshape(1, SGU_WIDTH), ws, bs_full, wa)


HALO = 16
CONV_COLS = 512


def _xbc_conv_kernel(h_ref, prev_ref, next_ref, w_ref, wdt_ref, cw_ref, cb_ref, o_ref, dt_ref, ext_ref, ext2_ref):
    i = pl.program_id(1)
    last = pl.num_programs(1) - 1
    tl = h_ref.shape[1]
    rows = jnp.concatenate([prev_ref[0], h_ref[0], next_ref[0]], axis=0)
    ext_refs = (ext_ref, ext2_ref)
    width = ext_ref.shape[1]
    for c in range(w_ref.shape[1] // width):
        cols = slice(c * width, (c + 1) * width)
        ext = ext_refs[c % 2]
        ext[...] = jnp.dot(rows, w_ref[:, cols], preferred_element_type=F32)
        ext[0:HALO, :] = jnp.where(i > 0, ext[0:HALO, :], 0.0)
        ext[HALO + tl:, :] = jnp.where(i < last, ext[HALO + tl:, :], 0.0)
        acc = jnp.zeros((tl, width), F32) + cb_ref[:, cols]
        for k in range(SSD_CONV):
            start = HALO - SSD_CONV // 2 + k
            acc = acc + cw_ref[k:k + 1, cols] * ext[start:start + tl, :]
        o_ref[0, :, cols] = _silu(acc).astype(o_ref.dtype)
    dt_ref[0] = jnp.dot(h_ref[0], wdt_ref[...], preferred_element_type=F32)


def _proj_xbc_conv(hn, w_xbc, w_dt, conv_w, conv_b, tl=256):
    bsz, seq, _ = hn.shape
    ch = w_xbc.shape[1]
    tl = min(tl, seq)
    nblk = tl // HALO
    last_blk = seq // HALO - 1
    const = lambda b, i: (0, 0)
    return pl.pallas_call(
        _xbc_conv_kernel,
        out_shape=(jax.ShapeDtypeStruct((bsz, seq, ch), BF16), jax.ShapeDtypeStruct((bsz, seq, LANES), F32)),
        grid=(bsz, seq // tl),
        in_specs=[pl.BlockSpec((1, tl, D_MODEL), lambda b, i: (b, i, 0)),
                  pl.BlockSpec((1, HALO, D_MODEL), lambda b, i: (b, jnp.maximum(i * nblk - 1, 0), 0)),
                  pl.BlockSpec((1, HALO, D_MODEL), lambda b, i: (b, jnp.minimum((i + 1) * nblk, last_blk), 0)),
                  pl.BlockSpec((D_MODEL, ch), const),
                  pl.BlockSpec((D_MODEL, LANES), const),
                  pl.BlockSpec((SSD_CONV, ch), const),
                  pl.BlockSpec((1, ch), const)],
        out_specs=(pl.BlockSpec((1, tl, ch), lambda b, i: (b, i, 0)),
                   pl.BlockSpec((1, tl, LANES), lambda b, i: (b, i, 0))),
        scratch_shapes=[pltpu.VMEM((tl + 2 * HALO, CONV_COLS), F32), pltpu.VMEM((tl + 2 * HALO, CONV_COLS), F32)],
        compiler_params=_params(("parallel", "parallel")),
        name="proj_xbc_conv",
    )(hn, hn, hn, w_xbc, w_dt, conv_w, conv_b.reshape(1, ch))


def _ssd_kernel(xbc_ref, dt_ref, bias_ref, alog_ref, o_ref, state_ref, *, reverse):
    c = pl.program_id(1)

    @pl.when(c == 0)
    def _():
        state_ref[...] = jnp.zeros_like(state_ref)

    col0 = SSD_HEADS if reverse else 0
    row = lax.broadcasted_iota(jnp.int32, (CHUNK, CHUNK), 0)
    lane = lax.broadcasted_iota(jnp.int32, (CHUNK, CHUNK), 1)
    tri = (row <= lane) if reverse else (row >= lane)
    lo_half = lane < SSD_HEAD_DIM

    dt = _softplus(dt_ref[0] + bias_ref[...])
    adt = dt * (-jnp.exp(alog_ref[...]))
    cs = adt
    shift = 1
    while shift < CHUNK:
        if reverse:
            moved = pltpu.roll(cs, CHUNK - shift, axis=0)
            cs = cs + jnp.where(row < CHUNK - shift, moved, 0.0)
        else:
            moved = pltpu.roll(cs, shift, axis=0)
            cs = cs + jnp.where(row >= shift, moved, 0.0)
        shift *= 2
    cs_t = cs.T
    dt_t = dt.T
    end = 0 if reverse else CHUNK - 1
    cs_end = jnp.broadcast_to(cs_t[:, end:end + 1], (CHUNK, CHUNK))
    w_all = dt_t * jnp.exp(cs_end - cs_t)
    dec_all = jnp.exp(cs_end)

    hpg = SSD_HEADS // SSD_GROUPS
    for g in range(SSD_GROUPS):
        b_off = SSD_INNER + g * SSD_STATE
        c_off = SSD_INNER + SSD_GROUPS * SSD_STATE + g * SSD_STATE
        bg = xbc_ref[0, :, b_off:b_off + SSD_STATE]
        cg = xbc_ref[0, :, c_off:c_off + SSD_STATE]
        cb = lax.dot_general(cg, bg, (((1,), (1,)), ((), ())), preferred_element_type=F32)
        cg32 = cg.astype(F32)
        bg_t = bg.astype(F32).T
        for j in range(hpg // 2):
            pair = g * (hpg // 2) + j
            lhs_parts, lhs2_parts = [], []
            for k in range(2):
                col = col0 + 2 * pair + k
                lmat = jnp.broadcast_to(cs[:, col:col + 1], (CHUNK, CHUNK))
                decay = jnp.where(tri, jnp.exp(lmat - cs_t[col:col + 1, :]), 0.0)
                lhs_parts.append((cb * decay * dt_t[col:col + 1, :]).astype(BF16))
                lhs_parts.append((cg32 * jnp.exp(lmat)).astype(BF16))
                lhs2_parts.append((bg_t * w_all[col:col + 1, :]).astype(BF16))
            xs = xbc_ref[0, :, pair * LANES:(pair + 1) * LANES]
            zero = jnp.zeros_like(xs)
            x0 = jnp.where(lo_half, xs, zero)
            x1 = jnp.where(lo_half, zero, xs)
            st = state_ref[pair]
            st16 = st.astype(BF16)
            s0 = jnp.where(lo_half, st16, zero)
            s1 = jnp.where(lo_half, zero, st16)
            lhs = jnp.concatenate(lhs_parts, axis=1)
            rhs = jnp.concatenate([x0, s0, x1, s1], axis=0)
            o_ref[0, :, pair * LANES:(pair + 1) * LANES] = jnp.dot(
                lhs, rhs, preferred_element_type=F32).astype(o_ref.dtype)
            lhs2 = jnp.concatenate(lhs2_parts, axis=1)
            rhs2 = jnp.concatenate([x0, x1], axis=0)
            col_a = col0 + 2 * pair
            dec = jnp.where(lo_half, dec_all[col_a:col_a + 1, :], dec_all[col_a + 1:col_a + 2, :])
            state_ref[pair] = st * dec + jnp.dot(lhs2, rhs2, preferred_element_type=F32)


def _ssd_scan(xbc_act, dt_raw, dt_bias_row, alog_row, reverse):
    bsz, seq, _ = xbc_act.shape
    nc = seq // CHUNK
    if reverse:
        cmap = lambda b, c: (b, nc - 1 - c, 0)
    else:
        cmap = lambda b, c: (b, c, 0)
    return pl.pallas_call(
        functools.partial(_ssd_kernel, reverse=reverse),
        out_shape=jax.ShapeDtypeStruct((bsz, seq, SSD_INNER), BF16),
        grid=(bsz, nc),
        in_specs=[pl.BlockSpec((1, CHUNK, SSD_CONV_DIM), cmap),
                  pl.BlockSpec((1, CHUNK, LANES), cmap),
                  pl.BlockSpec((1, LANES), lambda b, c: (0, 0)),
                  pl.BlockSpec((1, LANES), lambda b, c: (0, 0))],
        out_specs=pl.BlockSpec((1, CHUNK, SSD_INNER), cmap),
        scratch_shapes=[pltpu.VMEM((SSD_HEADS // 2, SSD_STATE, LANES), F32)],
        compiler_params=_params(("parallel", "arbitrary")),
        name="ssd_bwd" if reverse else "ssd_fwd",
    )(xbc_act, dt_raw, dt_bias_row, alog_row)


def _tail_kernel(yf_ref, yb_ref, xs_ref, z_ref, gb_ref, pa_ref, x_ref, g1_ref, dskip_ref, gain_ref,
                 wb_ref, wo_ref, n2_ref, sh2_ref, sc2_ref, xo_ref, h2t_ref):
    y = yf_ref[...].astype(F32) + yb_ref[...].astype(F32) + dskip_ref[...] * xs_ref[...].astype(F32)
    y = y * z_ref[...].astype(F32)
    gw = SSD_INNER // SSD_GROUPS
    parts = []
    for g in range(SSD_GROUPS):
        yg = y[:, g * gw:(g + 1) * gw]
        parts.append(yg * lax.rsqrt(jnp.mean(yg * yg, axis=-1, keepdims=True) + EPS))
    yn = (jnp.concatenate(parts, axis=1) * gain_ref[...]).astype(BF16)
    pb = jnp.dot(yn, wb_ref[...], preferred_element_type=F32)
    merged = pa_ref[...].astype(F32) + gb_ref[...].astype(F32) * pb
    out = jnp.dot(merged.astype(BF16), wo_ref[...], preferred_element_type=F32)
    x = x_ref[...] + g1_ref[...] * out
    xo_ref[...] = x
    h = x * lax.rsqrt(jnp.mean(x * x, axis=-1, keepdims=True) + EPS) * n2_ref[...]
    h2t_ref[...] = (h * (1.0 + sc2_ref[...]) + sh2_ref[...]).T.astype(h2t_ref.dtype)


def _tail(yf, yb, xbc_act, z_act, gates, pa, x, mod, dskip_row, ssd_gain, wb, wo, norm2_gain, seq, tm=256):
    t = x.shape[0]
    tm = min(tm, seq)
    per_row = seq // tm
    row = lambda i: (i, 0)
    const = lambda i: (0, 0)
    modspec = lambda k: pl.BlockSpec((None, None, 1, D_MODEL), lambda i: (i // per_row, k, 0, 0))
    return pl.pallas_call(
        _tail_kernel,
        out_shape=(jax.ShapeDtypeStruct((t, D_MODEL), F32), jax.ShapeDtypeStruct((D_MODEL, t), BF16)),
        grid=(t // tm,),
        in_specs=[pl.BlockSpec((tm, SSD_INNER), row),
                  pl.BlockSpec((tm, SSD_INNER), row),
                  pl.BlockSpec((tm, SSD_INNER), row),
                  pl.BlockSpec((tm, SSD_INNER), row),
                  pl.BlockSpec((tm, D_MODEL), lambda i: (i, 1)),
                  pl.BlockSpec((tm, D_MODEL), row),
                  pl.BlockSpec((tm, D_MODEL), row),
                  modspec(2),
                  pl.BlockSpec((1, SSD_INNER), const),
                  pl.BlockSpec((1, SSD_INNER), const),
                  pl.BlockSpec((SSD_INNER, D_MODEL), const),
                  pl.BlockSpec((D_MODEL, D_MODEL), const),
                  pl.BlockSpec((1, D_MODEL), const),
                  modspec(3),
                  modspec(4)],
        out_specs=(pl.BlockSpec((tm, D_MODEL), row), pl.BlockSpec((D_MODEL, tm), lambda i: (0, i))),
        compiler_params=_params(("parallel",)),
        name="mix_tail",
    )(yf, yb, xbc_act, z_act, gates, pa, x, mod, dskip_row, ssd_gain.reshape(1, SSD_INNER), wb, wo,
      norm2_gain.reshape(1, D_MODEL), mod, mod)


CAND_PAIRS = [(a, b) for a in range(PEER_TOPK) for b in range(PEER_TOPK) if (a + 1) * (b + 1) <= PEER_TOPK]
RANK_OUTSIDE = float(2 * PEER_TOPK)
SUBLANES = 8


def _sorted_desc(cols):
    c = list(cols)
    n = len(c)
    k = 2
    while k <= n:
        j = k // 2
        while j >= 1:
            for i in range(n):
                l = i ^ j
                if l > i:
                    hi, lo = jnp.maximum(c[i], c[l]), jnp.minimum(c[i], c[l])
                    c[i], c[l] = (hi, lo) if (i & k) == 0 else (lo, hi)
            j //= 2
        k *= 2
    return c


def _top_values(s):
    c = _sorted_desc([s[v * SUBLANES:(v + 1) * SUBLANES, :] for v in range(PEER_NKEYS // SUBLANES)])
    tops = []
    for r in range(PEER_TOPK):
        m = jnp.max(c[0], axis=0, keepdims=True)
        tops.append(m)
        depth = PEER_TOPK - 1 - r
        if depth:
            hit = c[0] == m
            c = [jnp.where(hit, c[q + 1], c[q]) for q in range(depth)]
    return tops


def _count_at_least(x, thresholds):
    cnt = jnp.zeros(x.shape, F32)
    for n, th in enumerate(thresholds):
        cnt = jnp.where(x >= th, float(n + 1), cnt)
    return cnt


def _route_kernel(ht_ref, wq_ref, keys_ref, c0_ref, a_ref, r1_ref, b_ref, top_ref, s_ref):
    q_t = jnp.dot(wq_ref[...], ht_ref[...], preferred_element_type=F32)
    for hd in range(PEER_HEADS):
        for half in range(2):
            r0 = (hd * 2 + half) * PEER_HALF
            q = q_t[r0:r0 + PEER_HALF, :].astype(BF16)
            s = jnp.dot(keys_ref[half], q, preferred_element_type=F32)
            s_ref[half, hd] = s
            for r, m in enumerate(_top_values(s)):
                top_ref[half, r, hd:hd + 1, :] = m
    cands = [top_ref[0, a] + top_ref[1, b] for a, b in CAND_PAIRS]
    best = cands[0]
    work = list(cands)
    kth = best
    for r in range(PEER_TOPK + 1):
        prev = kth
        kth = work[0]
        for x in work[1:]:
            kth = jnp.maximum(kth, x)
        if r < PEER_TOPK:
            work = [jnp.where(x == kth, NEG_BIG, x) for x in work]
    tau = 0.5 * (prev + kth)
    z = jnp.zeros_like(best)
    for x in cands:
        z = z + jnp.where(x >= tau, jnp.exp(x - best), 0.0)
    inv_z = 1.0 / z
    for hd in range(PEER_HEADS):
        s0 = s_ref[0, hd]
        s1 = s_ref[1, hd]
        tau_h = tau[hd:hd + 1, :]
        tops1 = [top_ref[1, b, hd:hd + 1, :] for b in range(PEER_TOPK)]
        floor0 = top_ref[0, PEER_TOPK - 1, hd:hd + 1, :]
        c0_ref[hd] = _bf16_pair_words(_count_at_least(s0, [jnp.maximum(tau_h - t, floor0) for t in tops1]))
        reached = _count_at_least(s1, tops1[::-1])
        r1_ref[hd] = jnp.where(reached > 0.0, float(PEER_TOPK) - reached, RANK_OUTSIDE).astype(BF16)
        a_ref[hd] = _bf16_pair_words(jnp.exp(s0 - top_ref[0, 0, hd:hd + 1, :]))
        b_ref[hd] = (jnp.exp(s1 - tops1[0]) * inv_z[hd:hd + 1, :]).astype(BF16)


def _route(h2t, wq_t, keys, tt=256):
    t = h2t.shape[1]
    shp32 = jax.ShapeDtypeStruct((PEER_HEADS, PEER_NKEYS, t), jnp.uint32)
    shp16 = jax.ShapeDtypeStruct((PEER_HEADS, PEER_NKEYS, t), BF16)
    ospec = pl.BlockSpec((PEER_HEADS, PEER_NKEYS, tt), lambda i: (0, 0, i))
    return pl.pallas_call(
        _route_kernel,
        out_shape=(shp32, shp32, shp16, shp16),
        grid=(t // tt,),
        in_specs=[pl.BlockSpec((D_MODEL, tt), lambda i: (0, i)),
                  pl.BlockSpec((2 * PEER_HEADS * PEER_HALF, D_MODEL), lambda i: (0, 0)),
                  pl.BlockSpec((2, PEER_NKEYS, PEER_HALF), lambda i: (0, 0, 0))],
        out_specs=(ospec, ospec, ospec, ospec),
        scratch_shapes=[pltpu.VMEM((2, PEER_TOPK, PEER_HEADS, tt), F32),
                        pltpu.VMEM((2, PEER_HEADS, PEER_NKEYS, tt), F32)],
        compiler_params=_params(("parallel",)),
        name="peer_route",
    )(h2t, wq_t, keys)


E_BLK = 2048
I_BLK = E_BLK // PEER_NKEYS
E_SUB = 256
I_SUB = E_SUB // PEER_NKEYS
ROWS16 = 16


def _peer_kernel(ht_ref, u_ref, vt_ref, c0_ref, a_ref, r1_ref, b_ref, x_ref, g2_ref, gain_ref, sh_ref, sc_ref,
                 *refs, final_norm):
    n_out = 1 if final_norm else 2
    o_ref = refs[0]
    act0_ref, act1_ref, p_ref, acc_ref = refs[n_out:]
    e = pl.program_id(1)
    tt = ht_ref.shape[1]
    t_sub = act0_ref.shape[1]
    n_jp = PEER_NKEYS // ROWS16
    act_refs = (act0_ref, act1_ref)
    units = [(sb, th) for sb in range(E_BLK // E_SUB) for th in range(tt // t_sub)]

    @pl.when(e == 0)
    def _():
        acc_ref[...] = jnp.zeros_like(acc_ref)

    def activation_matmul(n):
        sb, th = units[n]
        act_refs[n % 2][...] = jnp.dot(u_ref[sb * E_SUB:(sb + 1) * E_SUB, :],
                                       ht_ref[:, th * t_sub:(th + 1) * t_sub],
                                       preferred_element_type=F32)

    def gated_activations(n):
        sb, th = units[n]
        act_ref = act_refs[n % 2]
        ils = [sb * I_SUB + k for k in range(I_SUB)]
        for tc in range(t_sub // LANES):
            ls = slice(tc * LANES, (tc + 1) * LANES)
            ts = slice(th * t_sub + tc * LANES, th * t_sub + (tc + 1) * LANES)
            accs = [[None] * n_jp for _ in ils]
            for hd in range(PEER_HEADS):
                cnt = [_row_as_bf16_tile(c0_ref[hd, il:il + 1, ts]) for il in ils]
                wgt = [_row_as_bf16_tile(a_ref[hd, il:il + 1, ts]) for il in ils]
                for jp in range(n_jp):
                    js = slice(jp * ROWS16, (jp + 1) * ROWS16)
                    r1 = r1_ref[hd, js, ts]
                    bb = b_ref[hd, js, ts]
                    for k in range(I_SUB):
                        term = jnp.minimum(jnp.maximum(cnt[k] - r1, 0.0), wgt[k]) * bb
                        accs[k][jp] = term if accs[k][jp] is None else accs[k][jp] + term
            for k in range(I_SUB):
                for jp in range(n_jp):
                    r0 = k * PEER_NKEYS + jp * ROWS16
                    gel = _gelu_sigmoid_form(act_ref[r0:r0 + ROWS16, ls].astype(BF16))
                    p_ref[sb * E_SUB + r0:sb * E_SUB + r0 + ROWS16, ts] = accs[k][jp] * gel

    activation_matmul(0)
    for n in range(len(units)):
        if n + 1 < len(units):
            activation_matmul(n + 1)
        gated_activations(n)
    acc_ref[...] += jnp.dot(vt_ref[...], p_ref[...], preferred_element_type=F32)

    @pl.when(e == pl.num_programs(1) - 1)
    def _():
        x = x_ref[...] + g2_ref[...] * acc_ref[...].T
        y = x * lax.rsqrt(jnp.mean(x * x, axis=-1, keepdims=True) + EPS) * gain_ref[...]
        if final_norm:
            o_ref[...] = y
        else:
            o_ref[...] = x
            refs[1][...] = (y * (1.0 + sc_ref[...]) + sh_ref[...]).astype(BF16)


def _peer(h2t, u16, vt16, c0, a, r1, b, x, mod, gain, mod_next, seq, final_norm, tt=512):
    t = h2t.shape[1]
    tt = min(tt, seq)
    per_row = seq // tt
    ne = PEER_EXPERTS // E_BLK
    tok = lambda i, e: (i, 0)
    blk_i = pl.BlockSpec((PEER_HEADS, I_BLK, tt), lambda i, e: (0, e, i))
    full_j = pl.BlockSpec((PEER_HEADS, PEER_NKEYS, tt), lambda i, e: (0, 0, i))
    modspec = lambda tbl_idx: pl.BlockSpec((None, None, 1, D_MODEL), lambda i, e: (i // per_row, tbl_idx, 0, 0))
    x_shape = jax.ShapeDtypeStruct((t, D_MODEL), F32)
    x_spec = pl.BlockSpec((tt, D_MODEL), tok)
    return pl.pallas_call(
        functools.partial(_peer_kernel, final_norm=final_norm),
        out_shape=x_shape if final_norm else (x_shape, jax.ShapeDtypeStruct((t, D_MODEL), BF16)),
        grid=(t // tt, ne),
        in_specs=[pl.BlockSpec((D_MODEL, tt), lambda i, e: (0, i)),
                  pl.BlockSpec((E_BLK, D_MODEL), lambda i, e: (e, 0)),
                  pl.BlockSpec((D_MODEL, E_BLK), lambda i, e: (0, e)),
                  blk_i, blk_i, full_j, full_j,
                  x_spec,
                  modspec(5),
                  pl.BlockSpec((1, D_MODEL), lambda i, e: (0, 0)),
                  modspec(0),
                  modspec(1)],
        out_specs=x_spec if final_norm else (x_spec, x_spec),
        scratch_shapes=[pltpu.VMEM((E_SUB, tt), F32), pltpu.VMEM((E_SUB, tt), F32),
                        pltpu.VMEM((E_BLK, tt), BF16),
                        pltpu.VMEM((D_MODEL, tt), F32)],
        compiler_params=_params(("parallel", "arbitrary")),
        name="peer_dense",
    )(h2t, u16, vt16, c0, a, r1, b, x, mod, gain.reshape(1, D_MODEL), mod_next, mod_next)


def _prepare_weights(w_in, w_spatial, b_spatial, dt_bias, a_log, d_skip, w_proj_a, w_proj_b, w_out,
                     w_query, sub_keys, expert_u, expert_v):
    pad = LANES - 2 * SSD_HEADS
    w = {
        "w_uv": w_in[:, :, OFF_UV:OFF_Z].astype(BF16),
        "w_z": w_in[:, :, OFF_Z:OFF_XBC].astype(BF16),
        "w_xbc": w_in[:, :, OFF_XBC:OFF_DT].astype(BF16),
        "w_dt": jnp.pad(w_in[:, :, OFF_DT:OFF_GATE], ((0, 0), (0, 0), (0, pad))).astype(BF16),
        "w_gate": w_in[:, :, OFF_GATE:OFF_END].astype(BF16),
        "ws": w_spatial.astype(BF16),
        "bs": jnp.broadcast_to(b_spatial[..., None], b_spatial.shape + (CHUNK,)),
        "dt_bias": jnp.pad(dt_bias.reshape(DEPTH, 1, 2 * SSD_HEADS), ((0, 0), (0, 0), (0, pad))),
        "a_log": jnp.pad(a_log.reshape(DEPTH, 1, 2 * SSD_HEADS), ((0, 0), (0, 0), (0, pad))),
        "d_skip": jnp.repeat(d_skip, SSD_HEAD_DIM, axis=1).reshape(DEPTH, 1, SSD_INNER),
        "wa": w_proj_a.astype(BF16),
        "wb": w_proj_b.astype(BF16),
        "wo": w_out.astype(BF16),
        "wq_t": jnp.swapaxes(w_query, 1, 2).astype(BF16),
        "keys": sub_keys.astype(BF16),
        "u": expert_u.astype(BF16),
        "vt": jnp.swapaxes(expert_v, 1, 2).astype(BF16),
    }
    return w


def _trunk(x, c, w_mod, b_mod, norm1_gain, norm2_gain, sgu_gain, conv_w, conv_b, ssd_gain, final_gain, w):
    bsz, seq, _ = x.shape
    t = bsz * seq
    mod_all = _modulation(c, w_mod, b_mod)
    xf = x.reshape(t, D_MODEL)
    hn = _prenorm(x, norm1_gain[0], mod_all[0], 0, 1).reshape(t, D_MODEL)
    for l in range(DEPTH):
        mod = mod_all[l]
        uv =_matmul_act(hn, w["w_uv"][l], _gelu, BF16, 512, "proj_uv")
        z_act = _matmul_act(hn, w["w_z"][l], _silu, BF16, 512, "proj_z")
        gates = _matmul_act(hn, w["w_gate"][l], _sigmoid, BF16, 512, "proj_gate")
        pa = _sgu(uv, gates, sgu_gain[l], w["ws"][l], w["bs"][l], w["wa"][l])
        xbc_act, dt3 = _proj_xbc_conv(hn.reshape(bsz, seq, D_MODEL), w["w_xbc"][l], w["w_dt"][l],
                                      conv_w[l], conv_b[l])
        yf = _ssd_scan(xbc_act, dt3, w["dt_bias"][l], w["a_log"][l], reverse=False)
        yb = _ssd_scan(xbc_act, dt3, w["dt_bias"][l], w["a_log"][l], reverse=True)
        xf, h2t = _tail(yf.reshape(t, SSD_INNER), yb.reshape(t, SSD_INNER),
                       xbc_act.reshape(t, SSD_CONV_DIM), z_act, gates, pa, xf, mod,
                       w["d_skip"][l], ssd_gain[l], w["wb"][l], w["wo"][l], norm2_gain[l], seq)
        c0, a, r1, b = _route(h2t, w["wq_t"][l], w["keys"][l])
        if l == DEPTH - 1:
            xf = _peer(h2t, w["u"][l], w["vt"][l], c0, a, r1, b, xf, mod, final_gain, mod, seq, final_norm=True)
        else:
            xf, hn = _peer(h2t, w["u"][l], w["vt"][l], c0, a, r1, b, xf, mod, norm1_gain[l + 1],
                           mod_all[l + 1], seq, final_norm=False)
    return xf.reshape(bsz, seq, D_MODEL)


def kernel(x_prompt, x_sample, c_prompt, c_sample, w_mod, b_mod, norm1_gain, norm2_gain, w_in, sgu_gain, w_spatial, b_spatial, conv_w, conv_b, dt_bias, a_log, d_skip, ssd_gain, w_proj_a, w_proj_b, w_out, w_query, sub_keys, expert_u, expert_v, final_gain):
    w = _prepare_weights(w_in, w_spatial, b_spatial, dt_bias, a_log, d_skip, w_proj_a, w_proj_b,
                         w_out, w_query, sub_keys, expert_u, expert_v)
    args = (w_mod, b_mod, norm1_gain, norm2_gain, sgu_gain, conv_w, conv_b, ssd_gain, final_gain, w)
    y_prompt = _trunk(x_prompt, c_prompt, *args)
    y_sample = _trunk(x_sample, c_sample, *args)
    return (y_prompt, y_sample)
```

```python
import functools

import jax
import jax.numpy as jnp
from jax import lax
from jax.experimental import pallas as pl
from jax.experimental.pallas import tpu as pltpu

F32 = jnp.float32
BF16 = jnp.bfloat16

D_MODEL = 1024
DEPTH = 4
CHUNK = 128
SGU_WIDTH = 1024
SGU_GROUPS = 8
SSD_INNER = 2048
SSD_HEAD_DIM = 64
SSD_HEADS = 32
SSD_GROUPS = 4
SSD_STATE = 128
SSD_CONV = 5
SSD_CONV_DIM = 3072
PEER_HEADS = 8
PEER_NKEYS = 128
PEER_EXPERTS = PEER_NKEYS * PEER_NKEYS
PEER_HALF = 128
PEER_TOPK = 16
EPS = 1e-6

LANES = 128
MIB = 1024 * 1024
NEG_BIG = -3.0e38

OFF_UV, OFF_Z, OFF_XBC, OFF_DT, OFF_GATE, OFF_END = 0, 2048, 4096, 7168, 7232, 9280


def _params(semantics, vmem_mib=48):
    return pltpu.CompilerParams(dimension_semantics=semantics, vmem_limit_bytes=vmem_mib * MIB)


def _sigmoid(x):
    return 1.0 / (1.0 + jnp.exp(-x))


def _silu(x):
    return x * _sigmoid(x)


def _gelu(x):
    return 0.5 * x * (1.0 + jnp.tanh(0.7978845608028654 * (x + 0.044715 * (x * x * x))))


def _gelu_sigmoid_form(x):
    t = (x * x) * (-2.0 * 0.7978845608028654 * 0.044715) + (-2.0 * 0.7978845608028654)
    return x / (1.0 + jnp.exp(x * t))


def _bf16_pair_words(x):
    u = pltpu.bitcast(x.astype(BF16).astype(F32), jnp.uint32)
    return u | (u >> 16)


def _row_as_bf16_tile(words):
    return pltpu.bitcast(jnp.broadcast_to(words, (8, LANES)), BF16)


def _softplus(x):
    return jnp.maximum(x, 0.0) + jnp.log(1.0 + jnp.exp(-jnp.abs(x)))


def _mod_kernel(c_ref, w_ref, b_ref, o_ref):
    c = c_ref[...]
    o_ref[0] = jnp.dot(_silu(c), w_ref[0], preferred_element_type=F32,
                       precision=lax.Precision.HIGHEST) + b_ref[0]


def _modulation(c, w_mod, b_mod):
    bsz = c.shape[0]
    bp = -(-bsz // 8) * 8
    cp = jnp.pad(c, ((0, bp - bsz), (0, 0)))
    out = pl.pallas_call(
        _mod_kernel,
        out_shape=jax.ShapeDtypeStruct((DEPTH, bp, 6 * D_MODEL), F32),
        grid=(DEPTH, 6),
        in_specs=[pl.BlockSpec((bp, D_MODEL), lambda l, j: (0, 0)),
                  pl.BlockSpec((1, D_MODEL, D_MODEL), lambda l, j: (l, 0, j)),
                  pl.BlockSpec((1, 1, D_MODEL), lambda l, j: (l, 0, j))],
        out_specs=pl.BlockSpec((1, bp, D_MODEL), lambda l, j: (l, 0, j)),
        compiler_params=_params(("parallel", "parallel")),
        name="modulation",
    )(cp, w_mod, b_mod.reshape(DEPTH, 1, 6 * D_MODEL))
    return out[:, :bsz].reshape(DEPTH, bsz, 6, 1, D_MODEL)


def _prenorm_kernel(x_ref, gain_ref, sh_ref, sc_ref, o_ref):
    x = x_ref[0]
    y = x * lax.rsqrt(jnp.mean(x * x, axis=-1, keepdims=True) + EPS) * gain_ref[...]
    o_ref[0] = (y * (1.0 + sc_ref[...]) + sh_ref[...]).astype(o_ref.dtype)


def _prenorm(x, gain, mod, shift_idx, scale_idx, tl=512):
    bsz, seq, _ = x.shape
    tl = min(tl, seq)
    return pl.pallas_call(
        _prenorm_kernel,
        out_shape=jax.ShapeDtypeStruct((bsz, seq, D_MODEL), BF16),
        grid=(bsz, seq // tl),
        in_specs=[pl.BlockSpec((1, tl, D_MODEL), lambda b, i: (b, i, 0)),
                  pl.BlockSpec((1, D_MODEL), lambda b, i: (0, 0)),
                  pl.BlockSpec((None, None, 1, D_MODEL), lambda b, i: (b, shift_idx, 0, 0)),
                  pl.BlockSpec((None, None, 1, D_MODEL), lambda b, i: (b, scale_idx, 0, 0))],
        out_specs=pl.BlockSpec((1, tl, D_MODEL), lambda b, i: (b, i, 0)),
        compiler_params=_params(("parallel", "parallel")),
        name="prenorm",
    )(x, gain.reshape(1, D_MODEL), mod, mod)


def _matmul_kernel(a_ref, w_ref, o_ref, *, act):
    acc = jnp.dot(a_ref[...], w_ref[...], preferred_element_type=F32)
    o_ref[...] = act(acc).astype(o_ref.dtype)


def _matmul_act(a, w, act, out_dtype, tm, name):
    m, k = a.shape
    n = w.shape[1]
    tm = min(tm, m)
    return pl.pallas_call(
        functools.partial(_matmul_kernel, act=act),
        out_shape=jax.ShapeDtypeStruct((m, n), out_dtype),
        grid=(m // tm,),
        in_specs=[pl.BlockSpec((tm, k), lambda i: (i, 0)),
                  pl.BlockSpec((k, n), lambda i: (0, 0))],
        out_specs=pl.BlockSpec((tm, n), lambda i: (i, 0)),
        compiler_params=_params(("parallel",)),
        name=name,
    )(a, w)


def _sgu_kernel(uv_ref, ga_ref, gain_ref, ws_ref, bs_ref, wa_ref, o_ref, ya_ref):
    rows = uv_ref.shape[0]
    v = uv_ref[:, SGU_WIDTH:].astype(F32)
    vn = (v * lax.rsqrt(jnp.mean(v * v, axis=-1, keepdims=True) + EPS) * gain_ref[...]).astype(BF16)
    gdim = SGU_WIDTH // SGU_GROUPS
    for n in range(rows // CHUNK):
        r0 = n * CHUNK
        for g in range(SGU_GROUPS):
            c0 = g * gdim
            mixed = jnp.dot(ws_ref[g], vn[r0:r0 + CHUNK, c0:c0 + gdim],
                            preferred_element_type=F32) + bs_ref[g]
            u = uv_ref[r0:r0 + CHUNK, c0:c0 + gdim].astype(F32)
            ya_ref[r0:r0 + CHUNK, c0:c0 + gdim] = (u * mixed).astype(BF16)
    pa = jnp.dot(ya_ref[...], wa_ref[...], preferred_element_type=F32)
    o_ref[...] = (ga_ref[...].astype(F32) * pa).astype(o_ref.dtype)


def _sgu(uv, gates, sgu_gain, ws, bs_full, wa, ts=256):
    t = uv.shape[0]
    return pl.pallas_call(
        _sgu_kernel,
        out_shape=jax.ShapeDtypeStruct((t, D_MODEL), BF16),
        grid=(t // ts,),
        in_specs=[pl.BlockSpec((ts, 2 * SGU_WIDTH), lambda i: (i, 0)),
                  pl.BlockSpec((ts, D_MODEL), lambda i: (i, 0)),
                  pl.BlockSpec((1, SGU_WIDTH), lambda i: (0, 0)),
                  pl.BlockSpec((SGU_GROUPS, CHUNK, CHUNK), lambda i: (0, 0, 0)),
                  pl.BlockSpec((SGU_GROUPS, CHUNK, CHUNK), lambda i: (0, 0, 0)),
                  pl.BlockSpec((SGU_WIDTH, D_MODEL), lambda i: (0, 0))],
        out_specs=pl.BlockSpec((ts, D_MODEL), lambda i: (i, 0)),
        scratch_shapes=[pltpu.VMEM((ts, SGU_WIDTH), BF16)],
        compiler_params=_params(("parallel",)),
        name="sgu_proj_a",
    )(uv, gates, sgu_gain.reshape(1, SGU_WIDTH), ws, bs_full, wa)


HALO = 16
CONV_COLS = 512


def _xbc_conv_kernel(h_ref, prev_ref, next_ref, w_ref, wdt_ref, cw_ref, cb_ref, o_ref, dt_ref, ext_ref, ext2_ref):
    i = pl.program_id(1)
    last = pl.num_programs(1) - 1
    tl = h_ref.shape[1]
    rows = jnp.concatenate([prev_ref[0], h_ref[0], next_ref[0]], axis=0)
    ext_refs = (ext_ref, ext2_ref)
    width = ext_ref.shape[1]
    for c in range(w_ref.shape[1] // width):
        cols = slice(c * width, (c + 1) * width)
        ext = ext_refs[c % 2]
        ext[...] = jnp.dot(rows, w_ref[:, cols], preferred_element_type=F32)
        ext[0:HALO, :] = jnp.where(i > 0, ext[0:HALO, :], 0.0)
        ext[HALO + tl:, :] = jnp.where(i < last, ext[HALO + tl:, :], 0.0)
        acc = jnp.zeros((tl, width), F32) + cb_ref[:, cols]
        for k in range(SSD_CONV):
            start = HALO - SSD_CONV // 2 + k
            acc = acc + cw_ref[k:k + 1, cols] * ext[start:start + tl, :]
        o_ref[0, :, cols] = _silu(acc).astype(o_ref.dtype)
    dt_ref[0] = jnp.dot(h_ref[0], wdt_ref[...], preferred_element_type=F32)


def _proj_xbc_conv(hn, w_xbc, w_dt, conv_w, conv_b, tl=256):
    bsz, seq, _ = hn.shape
    ch = w_xbc.shape[1]
    tl = min(tl, seq)
    nblk = tl // HALO
    last_blk = seq // HALO - 1
    const = lambda b, i: (0, 0)
    return pl.pallas_call(
        _xbc_conv_kernel,
        out_shape=(jax.ShapeDtypeStruct((bsz, seq, ch), BF16), jax.ShapeDtypeStruct((bsz, seq, LANES), F32)),
        grid=(bsz, seq // tl),
        in_specs=[pl.BlockSpec((1, tl, D_MODEL), lambda b, i: (b, i, 0)),
                  pl.BlockSpec((1, HALO, D_MODEL), lambda b, i: (b, jnp.maximum(i * nblk - 1, 0), 0)),
                  pl.BlockSpec((1, HALO, D_MODEL), lambda b, i: (b, jnp.minimum((i + 1) * nblk, last_blk), 0)),
                  pl.BlockSpec((D_MODEL, ch), const),
                  pl.BlockSpec((D_MODEL, LANES), const),
                  pl.BlockSpec((SSD_CONV, ch), const),
                  pl.BlockSpec((1, ch), const)],
        out_specs=(pl.BlockSpec((1, tl, ch), lambda b, i: (b, i, 0)),
                   pl.BlockSpec((1, tl, LANES), lambda b, i: (b, i, 0))),
        scratch_shapes=[pltpu.VMEM((tl + 2 * HALO, CONV_COLS), F32), pltpu.VMEM((tl + 2 * HALO, CONV_COLS), F32)],
        compiler_params=_params(("parallel", "parallel")),
        name="proj_xbc_conv",
    )(hn, hn, hn, w_xbc, w_dt, conv_w, conv_b.reshape(1, ch))


def _ssd_kernel(xbc_ref, dt_ref, bias_ref, alog_ref, o_ref, state_ref, *, reverse):
    c = pl.program_id(1)

    @pl.when(c == 0)
    def _():
        state_ref[...] = jnp.zeros_like(state_ref)

    col0 = SSD_HEADS if reverse else 0
    row = lax.broadcasted_iota(jnp.int32, (CHUNK, CHUNK), 0)
    lane = lax.broadcasted_iota(jnp.int32, (CHUNK, CHUNK), 1)
    tri = (row <= lane) if reverse else (row >= lane)
    lo_half = lane < SSD_HEAD_DIM

    dt = _softplus(dt_ref[0] + bias_ref[...])
    adt = dt * (-jnp.exp(alog_ref[...]))
    cs = adt
    shift = 1
    while shift < CHUNK:
        if reverse:
            moved = pltpu.roll(cs, CHUNK - shift, axis=0)
            cs = cs + jnp.where(row < CHUNK - shift, moved, 0.0)
        else:
            moved = pltpu.roll(cs, shift, axis=0)
            cs = cs + jnp.where(row >= shift, moved, 0.0)
        shift *= 2
    cs_t = cs.T
    dt_t = dt.T
    end = 0 if reverse else CHUNK - 1
    cs_end = jnp.broadcast_to(cs_t[:, end:end + 1], (CHUNK, CHUNK))
    w_all = dt_t * jnp.exp(cs_end - cs_t)
    dec_all = jnp.exp(cs_end)

    hpg = SSD_HEADS // SSD_GROUPS
    for g in range(SSD_GROUPS):
        b_off = SSD_INNER + g * SSD_STATE
        c_off = SSD_INNER + SSD_GROUPS * SSD_STATE + g * SSD_STATE
        bg = xbc_ref[0, :, b_off:b_off + SSD_STATE]
        cg = xbc_ref[0, :, c_off:c_off + SSD_STATE]
        cb = lax.dot_general(cg, bg, (((1,), (1,)), ((), ())), preferred_element_type=F32)
        cg32 = cg.astype(F32)
        bg_t = bg.astype(F32).T
        for j in range(hpg // 2):
            pair = g * (hpg // 2) + j
            lhs_parts, lhs2_parts = [], []
            for k in range(2):
                col = col0 + 2 * pair + k
                lmat = jnp.broadcast_to(cs[:, col:col + 1], (CHUNK, CHUNK))
                decay = jnp.where(tri, jnp.exp(lmat - cs_t[col:col + 1, :]), 0.0)
                lhs_parts.append((cb * decay * dt_t[col:col + 1, :]).astype(BF16))
                lhs_parts.append((cg32 * jnp.exp(lmat)).astype(BF16))
                lhs2_parts.append((bg_t * w_all[col:col + 1, :]).astype(BF16))
            xs = xbc_ref[0, :, pair * LANES:(pair + 1) * LANES]
            zero = jnp.zeros_like(xs)
            x0 = jnp.where(lo_half, xs, zero)
            x1 = jnp.where(lo_half, zero, xs)
            st = state_ref[pair]
            st16 = st.astype(BF16)
            s0 = jnp.where(lo_half, st16, zero)
            s1 = jnp.where(lo_half, zero, st16)
            lhs = jnp.concatenate(lhs_parts, axis=1)
            rhs = jnp.concatenate([x0, s0, x1, s1], axis=0)
            o_ref[0, :, pair * LANES:(pair + 1) * LANES] = jnp.dot(
                lhs, rhs, preferred_element_type=F32).astype(o_ref.dtype)
            lhs2 = jnp.concatenate(lhs2_parts, axis=1)
            rhs2 = jnp.concatenate([x0, x1], axis=0)
            col_a = col0 + 2 * pair
            dec = jnp.where(lo_half, dec_all[col_a:col_a + 1, :], dec_all[col_a + 1:col_a + 2, :])
            state_ref[pair] = st * dec + jnp.dot(lhs2, rhs2, preferred_element_type=F32)


def _ssd_scan(xbc_act, dt_raw, dt_bias_row, alog_row, reverse):
    bsz, seq, _ = xbc_act.shape
    nc = seq // CHUNK
    if reverse:
        cmap = lambda b, c: (b, nc - 1 - c, 0)
    else:
        cmap = lambda b, c: (b, c, 0)
    return pl.pallas_call(
        functools.partial(_ssd_kernel, reverse=reverse),
        out_shape=jax.ShapeDtypeStruct((bsz, seq, SSD_INNER), BF16),
        grid=(bsz, nc),
        in_specs=[pl.BlockSpec((1, CHUNK, SSD_CONV_DIM), cmap),
                  pl.BlockSpec((1, CHUNK, LANES), cmap),
                  pl.BlockSpec((1, LANES), lambda b, c: (0, 0)),
                  pl.BlockSpec((1, LANES), lambda b, c: (0, 0))],
        out_specs=pl.BlockSpec((1, CHUNK, SSD_INNER), cmap),
        scratch_shapes=[pltpu.VMEM((SSD_HEADS // 2, SSD_STATE, LANES), F32)],
        compiler_params=_params(("parallel", "arbitrary")),
        name="ssd_bwd" if reverse else "ssd_fwd",
    )(xbc_act, dt_raw, dt_bias_row, alog_row)


def _tail_kernel(yf_ref, yb_ref, xs_ref, z_ref, gb_ref, pa_ref, x_ref, g1_ref, dskip_ref, gain_ref,
                 wb_ref, wo_ref, n2_ref, sh2_ref, sc2_ref, xo_ref, h2t_ref):
    y = yf_ref[...].astype(F32) + yb_ref[...].astype(F32) + dskip_ref[...] * xs_ref[...].astype(F32)
    y = y * z_ref[...].astype(F32)
    gw = SSD_INNER // SSD_GROUPS
    parts = []
    for g in range(SSD_GROUPS):
        yg = y[:, g * gw:(g + 1) * gw]
        parts.append(yg * lax.rsqrt(jnp.mean(yg * yg, axis=-1, keepdims=True) + EPS))
    yn = (jnp.concatenate(parts, axis=1) * gain_ref[...]).astype(BF16)
    pb = jnp.dot(yn, wb_ref[...], preferred_element_type=F32)
    merged = pa_ref[...].astype(F32) + gb_ref[...].astype(F32) * pb
    out = jnp.dot(merged.astype(BF16), wo_ref[...], preferred_element_type=F32)
    x = x_ref[...] + g1_ref[...] * out
    xo_ref[...] = x
    h = x * lax.rsqrt(jnp.mean(x * x, axis=-1, keepdims=True) + EPS) * n2_ref[...]
    h2t_ref[...] = (h * (1.0 + sc2_ref[...]) + sh2_ref[...]).T.astype(h2t_ref.dtype)


def _tail(yf, yb, xbc_act, z_act, gates, pa, x, mod, dskip_row, ssd_gain, wb, wo, norm2_gain, seq, tm=256):
    t = x.shape[0]
    tm = min(tm, seq)
    per_row = seq // tm
    row = lambda i: (i, 0)
    const = lambda i: (0, 0)
    modspec = lambda k: pl.BlockSpec((None, None, 1, D_MODEL), lambda i: (i // per_row, k, 0, 0))
    return pl.pallas_call(
        _tail_kernel,
        out_shape=(jax.ShapeDtypeStruct((t, D_MODEL), F32), jax.ShapeDtypeStruct((D_MODEL, t), BF16)),
        grid=(t // tm,),
        in_specs=[pl.BlockSpec((tm, SSD_INNER), row),
                  pl.BlockSpec((tm, SSD_INNER), row),
                  pl.BlockSpec((tm, SSD_INNER), row),
                  pl.BlockSpec((tm, SSD_INNER), row),
                  pl.BlockSpec((tm, D_MODEL), lambda i: (i, 1)),
                  pl.BlockSpec((tm, D_MODEL), row),
                  pl.BlockSpec((tm, D_MODEL), row),
                  modspec(2),
                  pl.BlockSpec((1, SSD_INNER), const),
                  pl.BlockSpec((1, SSD_INNER), const),
                  pl.BlockSpec((SSD_INNER, D_MODEL), const),
                  pl.BlockSpec((D_MODEL, D_MODEL), const),
                  pl.BlockSpec((1, D_MODEL), const),
                  modspec(3),
                  modspec(4)],
        out_specs=(pl.BlockSpec((tm, D_MODEL), row), pl.BlockSpec((D_MODEL, tm), lambda i: (0, i))),
        compiler_params=_params(("parallel",)),
        name="mix_tail",
    )(yf, yb, xbc_act, z_act, gates, pa, x, mod, dskip_row, ssd_gain.reshape(1, SSD_INNER), wb, wo,
      norm2_gain.reshape(1, D_MODEL), mod, mod)


CAND_PAIRS = [(a, b) for a in range(PEER_TOPK) for b in range(PEER_TOPK) if (a + 1) * (b + 1) <= PEER_TOPK]
RANK_OUTSIDE = float(2 * PEER_TOPK)
SUBLANES = 8


def _sorted_desc(cols):
    c = list(cols)
    n = len(c)
    k = 2
    while k <= n:
        j = k // 2
        while j >= 1:
            for i in range(n):
                l = i ^ j
                if l > i:
                    hi, lo = jnp.maximum(c[i], c[l]), jnp.minimum(c[i], c[l])
                    c[i], c[l] = (hi, lo) if (i & k) == 0 else (lo, hi)
            j //= 2
        k *= 2
    return c


def _top_values(s):
    c = _sorted_desc([s[v * SUBLANES:(v + 1) * SUBLANES, :] for v in range(PEER_NKEYS // SUBLANES)])
    tops = []
    for r in range(PEER_TOPK):
        m = jnp.max(c[0], axis=0, keepdims=True)
        tops.append(m)
        depth = PEER_TOPK - 1 - r
        if depth:
            hit = c[0] == m
            c = [jnp.where(hit, c[q + 1], c[q]) for q in range(depth)]
    return tops


def _count_at_least(x, thresholds):
    cnt = jnp.zeros(x.shape, F32)
    for n, th in enumerate(thresholds):
        cnt = jnp.where(x >= th, float(n + 1), cnt)
    return cnt


def _route_kernel(ht_ref, wq_ref, keys_ref, c0_ref, a_ref, r1_ref, b_ref, top_ref, s_ref):
    q_t = jnp.dot(wq_ref[...], ht_ref[...], preferred_element_type=F32)
    for hd in range(PEER_HEADS):
        for half in range(2):
            r0 = (hd * 2 + half) * PEER_HALF
            q = q_t[r0:r0 + PEER_HALF, :].astype(BF16)
            s = jnp.dot(keys_ref[half], q, preferred_element_type=F32)
            s_ref[half, hd] = s
            for r, m in enumerate(_top_values(s)):
                top_ref[half, r, hd:hd + 1, :] = m
    cands = [top_ref[0, a] + top_ref[1, b] for a, b in CAND_PAIRS]
    best = cands[0]
    work = list(cands)
    kth = best
    for r in range(PEER_TOPK + 1):
        prev = kth
        kth = work[0]
        for x in work[1:]:
            kth = jnp.maximum(kth, x)
        if r < PEER_TOPK:
            work = [jnp.where(x == kth, NEG_BIG, x) for x in work]
    tau = 0.5 * (prev + kth)
    z = jnp.zeros_like(best)
    for x in cands:
        z = z + jnp.where(x >= tau, jnp.exp(x - best), 0.0)
    inv_z = 1.0 / z
    for hd in range(PEER_HEADS):
        s0 = s_ref[0, hd]
        s1 = s_ref[1, hd]
        tau_h = tau[hd:hd + 1, :]
        tops1 = [top_ref[1, b, hd:hd + 1, :] for b in range(PEER_TOPK)]
        floor0 = top_ref[0, PEER_TOPK - 1, hd:hd + 1, :]
        c0_ref[hd] = _bf16_pair_words(_count_at_least(s0, [jnp.maximum(tau_h - t, floor0) for t in tops1]))
        reached = _count_at_least(s1, tops1[::-1])
        r1_ref[hd] = jnp.where(reached > 0.0, float(PEER_TOPK) - reached, RANK_OUTSIDE).astype(BF16)
        a_ref[hd] = _bf16_pair_words(jnp.exp(s0 - top_ref[0, 0, hd:hd + 1, :]))
        b_ref[hd] = (jnp.exp(s1 - tops1[0]) * inv_z[hd:hd + 1, :]).astype(BF16)


def _route(h2t, wq_t, keys, tt=256):
    t = h2t.shape[1]
    shp32 = jax.ShapeDtypeStruct((PEER_HEADS, PEER_NKEYS, t), jnp.uint32)
    shp16 = jax.ShapeDtypeStruct((PEER_HEADS, PEER_NKEYS, t), BF16)
    ospec = pl.BlockSpec((PEER_HEADS, PEER_NKEYS, tt), lambda i: (0, 0, i))
    return pl.pallas_call(
        _route_kernel,
        out_shape=(shp32, shp32, shp16, shp16),
        grid=(t // tt,),
        in_specs=[pl.BlockSpec((D_MODEL, tt), lambda i: (0, i)),
                  pl.BlockSpec((2 * PEER_HEADS * PEER_HALF, D_MODEL), lambda i: (0, 0)),
                  pl.BlockSpec((2, PEER_NKEYS, PEER_HALF), lambda i: (0, 0, 0))],
        out_specs=(ospec, ospec, ospec, ospec),
        scratch_shapes=[pltpu.VMEM((2, PEER_TOPK, PEER_HEADS, tt), F32),
                        pltpu.VMEM((2, PEER_HEADS, PEER_NKEYS, tt), F32)],
        compiler_params=_params(("parallel",)),
        name="peer_route",
    )(h2t, wq_t, keys)


E_BLK = 2048
I_BLK = E_BLK // PEER_NKEYS
E_SUB = 256
I_SUB = E_SUB // PEER_NKEYS
ROWS16 = 16
ACT_ROWS = 512


def _peer_kernel(ht_ref, u_ref, vt_ref, c0_ref, a_ref, r1_ref, b_ref, x_ref, g2_ref, gain_ref, sh_ref, sc_ref,
                 *refs, final_norm):
    n_out = 1 if final_norm else 2
    o_ref = refs[0]
    act0_ref, act1_ref, p_ref, acc_ref = refs[n_out:]
    e = pl.program_id(1)
    tt = ht_ref.shape[1]
    t_sub = act0_ref.shape[1]
    n_jp = PEER_NKEYS // ROWS16
    act_refs = (act0_ref, act1_ref)
    units = [(sb, th) for sb in range(E_BLK // E_SUB) for th in range(tt // t_sub)]

    @pl.when(e == 0)
    def _():
        acc_ref[...] = jnp.zeros_like(acc_ref)

    per_mm = act0_ref.shape[0] // E_SUB

    def activation_matmul(n):
        if n % per_mm:
            return
        sb, th = units[n]
        act_refs[(n // per_mm) % 2][...] = jnp.dot(u_ref[sb * E_SUB:(sb + per_mm) * E_SUB, :],
                                                   ht_ref[:, th * t_sub:(th + 1) * t_sub],
                                                   preferred_element_type=F32)

    def gated_activations(n):
        sb, th = units[n]
        act_ref = act_refs[(n // per_mm) % 2]
        a0 = (n % per_mm) * E_SUB
        ils = [sb * I_SUB + k for k in range(I_SUB)]
        for tc in range(t_sub // LANES):
            ls = slice(tc * LANES, (tc + 1) * LANES)
            ts = slice(th * t_sub + tc * LANES, th * t_sub + (tc + 1) * LANES)
            accs = [[None] * n_jp for _ in ils]
            for hd in range(PEER_HEADS):
                cnt = [_row_as_bf16_tile(c0_ref[hd, il:il + 1, ts]) for il in ils]
                wgt = [_row_as_bf16_tile(a_ref[hd, il:il + 1, ts]) for il in ils]
                for jp in range(n_jp):
                    js = slice(jp * ROWS16, (jp + 1) * ROWS16)
                    r1 = r1_ref[hd, js, ts]
                    bb = b_ref[hd, js, ts]
                    for k in range(I_SUB):
                        term = jnp.minimum(jnp.maximum(cnt[k] - r1, 0.0), wgt[k]) * bb
                        accs[k][jp] = term if accs[k][jp] is None else accs[k][jp] + term
            for k in range(I_SUB):
                for jp in range(n_jp):
                    r0 = k * PEER_NKEYS + jp * ROWS16
                    gel = _gelu_sigmoid_form(act_ref[a0 + r0:a0 + r0 + ROWS16, ls].astype(BF16))
                    p_ref[sb * E_SUB + r0:sb * E_SUB + r0 + ROWS16, ts] = accs[k][jp] * gel

    activation_matmul(0)
    for n in range(len(units)):
        if n % per_mm == 0 and n + per_mm < len(units):
            activation_matmul(n + per_mm)
        gated_activations(n)
    acc_ref[...] += jnp.dot(vt_ref[...], p_ref[...], preferred_element_type=F32)

    @pl.when(e == pl.num_programs(1) - 1)
    def _():
        x = x_ref[...] + g2_ref[...] * acc_ref[...].T
        y = x * lax.rsqrt(jnp.mean(x * x, axis=-1, keepdims=True) + EPS) * gain_ref[...]
        if final_norm:
            o_ref[...] = y
        else:
            o_ref[...] = x
            refs[1][...] = (y * (1.0 + sc_ref[...]) + sh_ref[...]).astype(BF16)


def _peer(h2t, u16, vt16, c0, a, r1, b, x, mod, gain, mod_next, seq, final_norm, tt=512):
    t = h2t.shape[1]
    tt = min(tt, seq)
    per_row = seq // tt
    ne = PEER_EXPERTS // E_BLK
    tok = lambda i, e: (i, 0)
    blk_i = pl.BlockSpec((PEER_HEADS, I_BLK, tt), lambda i, e: (0, e, i))
    full_j = pl.BlockSpec((PEER_HEADS, PEER_NKEYS, tt), lambda i, e: (0, 0, i))
    modspec = lambda tbl_idx: pl.BlockSpec((None, None, 1, D_MODEL), lambda i, e: (i // per_row, tbl_idx, 0, 0))
    x_shape = jax.ShapeDtypeStruct((t, D_MODEL), F32)
    x_spec = pl.BlockSpec((tt, D_MODEL), tok)
    return pl.pallas_call(
        functools.partial(_peer_kernel, final_norm=final_norm),
        out_shape=x_shape if final_norm else (x_shape, jax.ShapeDtypeStruct((t, D_MODEL), BF16)),
        grid=(t // tt, ne),
        in_specs=[pl.BlockSpec((D_MODEL, tt), lambda i, e: (0, i)),
                  pl.BlockSpec((E_BLK, D_MODEL), lambda i, e: (e, 0)),
                  pl.BlockSpec((D_MODEL, E_BLK), lambda i, e: (0, e)),
                  blk_i, blk_i, full_j, full_j,
                  x_spec,
                  modspec(5),
                  pl.BlockSpec((1, D_MODEL), lambda i, e: (0, 0)),
                  modspec(0),
                  modspec(1)],
        out_specs=x_spec if final_norm else (x_spec, x_spec),
        scratch_shapes=[pltpu.VMEM((ACT_ROWS, tt), F32), pltpu.VMEM((ACT_ROWS, tt), F32),
                        pltpu.VMEM((E_BLK, tt), BF16),
                        pltpu.VMEM((D_MODEL, tt), F32)],
        compiler_params=_params(("parallel", "arbitrary")),
        name="peer_dense",
    )(h2t, u16, vt16, c0, a, r1, b, x, mod, gain.reshape(1, D_MODEL), mod_next, mod_next)


def _prepare_weights(w_in, w_spatial, b_spatial, dt_bias, a_log, d_skip, w_proj_a, w_proj_b, w_out,
                     w_query, sub_keys, expert_u, expert_v):
    pad = LANES - 2 * SSD_HEADS
    w = {
        "w_uv": w_in[:, :, OFF_UV:OFF_Z].astype(BF16),
        "w_z": w_in[:, :, OFF_Z:OFF_XBC].astype(BF16),
        "w_xbc": w_in[:, :, OFF_XBC:OFF_DT].astype(BF16),
        "w_dt": jnp.pad(w_in[:, :, OFF_DT:OFF_GATE], ((0, 0), (0, 0), (0, pad))).astype(BF16),
        "w_gate": w_in[:, :, OFF_GATE:OFF_END].astype(BF16),
        "ws": w_spatial.astype(BF16),
        "bs": jnp.broadcast_to(b_spatial[..., None], b_spatial.shape + (CHUNK,)),
        "dt_bias": jnp.pad(dt_bias.reshape(DEPTH, 1, 2 * SSD_HEADS), ((0, 0), (0, 0), (0, pad))),
        "a_log": jnp.pad(a_log.reshape(DEPTH, 1, 2 * SSD_HEADS), ((0, 0), (0, 0), (0, pad))),
        "d_skip": jnp.repeat(d_skip, SSD_HEAD_DIM, axis=1).reshape(DEPTH, 1, SSD_INNER),
        "wa": w_proj_a.astype(BF16),
        "wb": w_proj_b.astype(BF16),
        "wo": w_out.astype(BF16),
        "wq_t": jnp.swapaxes(w_query, 1, 2).astype(BF16),
        "keys": sub_keys.astype(BF16),
        "u": expert_u.astype(BF16),
        "vt": jnp.swapaxes(expert_v, 1, 2).astype(BF16),
    }
    return w


def _trunk(x, c, w_mod, b_mod, norm1_gain, norm2_gain, sgu_gain, conv_w, conv_b, ssd_gain, final_gain, w):
    bsz, seq, _ = x.shape
    t = bsz * seq
    mod_all = _modulation(c, w_mod, b_mod)
    xf = x.reshape(t, D_MODEL)
    hn = _prenorm(x, norm1_gain[0], mod_all[0], 0, 1).reshape(t, D_MODEL)
    for l in range(DEPTH):
        mod = mod_all[l]
        uv =_matmul_act(hn, w["w_uv"][l], _gelu, BF16, 512, "proj_uv")
        z_act = _matmul_act(hn, w["w_z"][l], _silu, BF16, 512, "proj_z")
        gates = _matmul_act(hn, w["w_gate"][l], _sigmoid, BF16, 512, "proj_gate")
        pa = _sgu(uv, gates, sgu_gain[l], w["ws"][l], w["bs"][l], w["wa"][l])
        xbc_act, dt3 = _proj_xbc_conv(hn.reshape(bsz, seq, D_MODEL), w["w_xbc"][l], w["w_dt"][l],
                                      conv_w[l], conv_b[l])
        yf = _ssd_scan(xbc_act, dt3, w["dt_bias"][l], w["a_log"][l], reverse=False)
        yb = _ssd_scan(xbc_act, dt3, w["dt_bias"][l], w["a_log"][l], reverse=True)
        xf, h2t = _tail(yf.reshape(t, SSD_INNER), yb.reshape(t, SSD_INNER),
                       xbc_act.reshape(t, SSD_CONV_DIM), z_act, gates, pa, xf, mod,
                       w["d_skip"][l], ssd_gain[l], w["wb"][l], w["wo"][l], norm2_gain[l], seq)
        c0, a, r1, b = _route(h2t, w["wq_t"][l], w["keys"][l])
        if l == DEPTH - 1:
            xf = _peer(h2t, w["u"][l], w["vt"][l], c0, a, r1, b, xf, mod, final_gain, mod, seq, final_norm=True)
        else:
            xf, hn = _peer(h2t, w["u"][l], w["vt"][l], c0, a, r1, b, xf, mod, norm1_gain[l + 1],
                           mod_all[l + 1], seq, final_norm=False)
    return xf.reshape(bsz, seq, D_MODEL)


def kernel(x_prompt, x_sample, c_prompt, c_sample, w_mod, b_mod, norm1_gain, norm2_gain, w_in, sgu_gain, w_spatial, b_spatial, conv_w, conv_b, dt_bias, a_log, d_skip, ssd_gain, w_proj_a, w_proj_b, w_out, w_query, sub_keys, expert_u, expert_v, final_gain):
    w = _prepare_weights(w_in, w_spatial, b_spatial, dt_bias, a_log, d_skip, w_proj_a, w_proj_b,
                         w_out, w_query, sub_keys, expert_u, expert_v)
    args = (w_mod, b_mod, norm1_gain, norm2_gain, sgu_gain, conv_w, conv_b, ssd_gain, final_gain, w)
    y_prompt = _trunk(x_prompt, c_prompt, *args)
    y_sample = _trunk(x_sample, c_sample, *args)
    return (y_prompt, y_sample)
```

```python
import functools

import jax
import jax.numpy as jnp
from jax import lax
from jax.experimental import pallas as pl
from jax.experimental.pallas import tpu as pltpu

F32 = jnp.float32
BF16 = jnp.bfloat16

D_MODEL = 1024
DEPTH = 4
CHUNK = 128
SGU_WIDTH = 1024
SGU_GROUPS = 8
SSD_INNER = 2048
SSD_HEAD_DIM = 64
SSD_HEADS = 32
SSD_GROUPS = 4
SSD_STATE = 128
SSD_CONV = 5
SSD_CONV_DIM = 3072
PEER_HEADS = 8
PEER_NKEYS = 128
PEER_EXPERTS = PEER_NKEYS * PEER_NKEYS
PEER_HALF = 128
PEER_TOPK = 16
EPS = 1e-6

LANES = 128
MIB = 1024 * 1024
NEG_BIG = -3.0e38

OFF_UV, OFF_Z, OFF_XBC, OFF_DT, OFF_GATE, OFF_END = 0, 2048, 4096, 7168, 7232, 9280


def _params(semantics, vmem_mib=48):
    return pltpu.CompilerParams(dimension_semantics=semantics, vmem_limit_bytes=vmem_mib * MIB)


def _sigmoid(x):
    return 1.0 / (1.0 + jnp.exp(-x))


def _silu(x):
    return x * _sigmoid(x)


def _gelu(x):
    return 0.5 * x * (1.0 + jnp.tanh(0.7978845608028654 * (x + 0.044715 * (x * x * x))))


def _gelu_sigmoid_form(x):
    t = (x * x) * (-2.0 * 0.7978845608028654 * 0.044715) + (-2.0 * 0.7978845608028654)
    return x / (1.0 + jnp.exp(x * t))


def _bf16_pair_words(x):
    u = pltpu.bitcast(x.astype(BF16).astype(F32), jnp.uint32)
    return u | (u >> 16)


def _row_as_bf16_tile(words):
    return pltpu.bitcast(jnp.broadcast_to(words, (8, LANES)), BF16)


def _softplus(x):
    return jnp.maximum(x, 0.0) + jnp.log(1.0 + jnp.exp(-jnp.abs(x)))


def _mod_kernel(c_ref, w_ref, b_ref, o_ref):
    c = c_ref[...]
    o_ref[0] = jnp.dot(_silu(c), w_ref[0], preferred_element_type=F32,
                       precision=lax.Precision.HIGHEST) + b_ref[0]


def _modulation(c, w_mod, b_mod):
    bsz = c.shape[0]
    bp = -(-bsz // 8) * 8
    cp = jnp.pad(c, ((0, bp - bsz), (0, 0)))
    out = pl.pallas_call(
        _mod_kernel,
        out_shape=jax.ShapeDtypeStruct((DEPTH, bp, 6 * D_MODEL), F32),
        grid=(DEPTH, 6),
        in_specs=[pl.BlockSpec((bp, D_MODEL), lambda l, j: (0, 0)),
                  pl.BlockSpec((1, D_MODEL, D_MODEL), lambda l, j: (l, 0, j)),
                  pl.BlockSpec((1, 1, D_MODEL), lambda l, j: (l, 0, j))],
        out_specs=pl.BlockSpec((1, bp, D_MODEL), lambda l, j: (l, 0, j)),
        compiler_params=_params(("parallel", "parallel")),
        name="modulation",
    )(cp, w_mod, b_mod.reshape(DEPTH, 1, 6 * D_MODEL))
    return out[:, :bsz].reshape(DEPTH, bsz, 6, 1, D_MODEL)


def _prenorm_kernel(x_ref, gain_ref, sh_ref, sc_ref, o_ref):
    x = x_ref[0]
    y = x * lax.rsqrt(jnp.mean(x * x, axis=-1, keepdims=True) + EPS) * gain_ref[...]
    o_ref[0] = (y * (1.0 + sc_ref[...]) + sh_ref[...]).astype(o_ref.dtype)


def _prenorm(x, gain, mod, shift_idx, scale_idx, tl=512):
    bsz, seq, _ = x.shape
    tl = min(tl, seq)
    return pl.pallas_call(
        _prenorm_kernel,
        out_shape=jax.ShapeDtypeStruct((bsz, seq, D_MODEL), BF16),
        grid=(bsz, seq // tl),
        in_specs=[pl.BlockSpec((1, tl, D_MODEL), lambda b, i: (b, i, 0)),
                  pl.BlockSpec((1, D_MODEL), lambda b, i: (0, 0)),
                  pl.BlockSpec((None, None, 1, D_MODEL), lambda b, i: (b, shift_idx, 0, 0)),
                  pl.BlockSpec((None, None, 1, D_MODEL), lambda b, i: (b, scale_idx, 0, 0))],
        out_specs=pl.BlockSpec((1, tl, D_MODEL), lambda b, i: (b, i, 0)),
        compiler_params=_params(("parallel", "parallel")),
        name="prenorm",
    )(x, gain.reshape(1, D_MODEL), mod, mod)


def _matmul_kernel(a_ref, w_ref, o_ref, *, act):
    acc = jnp.dot(a_ref[...], w_ref[...], preferred_element_type=F32)
    o_ref[...] = act(acc).astype(o_ref.dtype)


def _matmul_act(a, w, act, out_dtype, tm, name):
    m, k = a.shape
    n = w.shape[1]
    tm = min(tm, m)
    return pl.pallas_call(
        functools.partial(_matmul_kernel, act=act),
        out_shape=jax.ShapeDtypeStruct((m, n), out_dtype),
        grid=(m // tm,),
        in_specs=[pl.BlockSpec((tm, k), lambda i: (i, 0)),
                  pl.BlockSpec((k, n), lambda i: (0, 0))],
        out_specs=pl.BlockSpec((tm, n), lambda i: (i, 0)),
        compiler_params=_params(("parallel",)),
        name=name,
    )(a, w)


def _sgu_kernel(uv_ref, ga_ref, gain_ref, ws_ref, bs_ref, wa_ref, o_ref, ya_ref):
    rows = uv_ref.shape[0]
    v = uv_ref[:, SGU_WIDTH:].astype(F32)
    vn = (v * lax.rsqrt(jnp.mean(v * v, axis=-1, keepdims=True) + EPS) * gain_ref[...]).astype(BF16)
    gdim = SGU_WIDTH // SGU_GROUPS
    for n in range(rows // CHUNK):
        r0 = n * CHUNK
        for g in range(SGU_GROUPS):
            c0 = g * gdim
            mixed = jnp.dot(ws_ref[g], vn[r0:r0 + CHUNK, c0:c0 + gdim],
                            preferred_element_type=F32) + bs_ref[g]
            u = uv_ref[r0:r0 + CHUNK, c0:c0 + gdim].astype(F32)
            ya_ref[r0:r0 + CHUNK, c0:c0 + gdim] = (u * mixed).astype(BF16)
    pa = jnp.dot(ya_ref[...], wa_ref[...], preferred_element_type=F32)
    o_ref[...] = (ga_ref[...].astype(F32) * pa).astype(o_ref.dtype)


def _sgu(uv, gates, sgu_gain, ws, bs_full, wa, ts=256):
    t = uv.shape[0]
    return pl.pallas_call(
        _sgu_kernel,
        out_shape=jax.ShapeDtypeStruct((t, D_MODEL), BF16),
        grid=(t // ts,),
        in_specs=[pl.BlockSpec((ts, 2 * SGU_WIDTH), lambda i: (i, 0)),
                  pl.BlockSpec((ts, D_MODEL), lambda i: (i, 0)),
                  pl.BlockSpec((1, SGU_WIDTH), lambda i: (0, 0)),
                  pl.BlockSpec((SGU_GROUPS, CHUNK, CHUNK), lambda i: (0, 0, 0)),
                  pl.BlockSpec((SGU_GROUPS, CHUNK, CHUNK), lambda i: (0, 0, 0)),
                  pl.BlockSpec((SGU_WIDTH, D_MODEL), lambda i: (0, 0))],
        out_specs=pl.BlockSpec((ts, D_MODEL), lambda i: (i, 0)),
        scratch_shapes=[pltpu.VMEM((ts, SGU_WIDTH), BF16)],
        compiler_params=_params(("parallel",)),
        name="sgu_proj_a",
    )(uv, gates, sgu_gain.reshape(1, SGU_WIDTH), ws, bs_full, wa)


HALO = 16
CONV_COLS = 512


def _xbc_conv_kernel(h_ref, prev_ref, next_ref, w_ref, wdt_ref, cw_ref, cb_ref, o_ref, dt_ref, ext_ref, ext2_ref):
    i = pl.program_id(1)
    last = pl.num_programs(1) - 1
    tl = h_ref.shape[1]
    rows = jnp.concatenate([prev_ref[0], h_ref[0], next_ref[0]], axis=0)
    ext_refs = (ext_ref, ext2_ref)
    width = ext_ref.shape[1]
    for c in range(w_ref.shape[1] // width):
        cols = slice(c * width, (c + 1) * width)
        ext = ext_refs[c % 2]
        ext[...] = jnp.dot(rows, w_ref[:, cols], preferred_element_type=F32)
        ext[0:HALO, :] = jnp.where(i > 0, ext[0:HALO, :], 0.0)
        ext[HALO + tl:, :] = jnp.where(i < last, ext[HALO + tl:, :], 0.0)
        acc = jnp.zeros((tl, width), F32) + cb_ref[:, cols]
        for k in range(SSD_CONV):
            start = HALO - SSD_CONV // 2 + k
            acc = acc + cw_ref[k:k + 1, cols] * ext[start:start + tl, :]
        o_ref[0, :, cols] = _silu(acc).astype(o_ref.dtype)
    dt_ref[0] = jnp.dot(h_ref[0], wdt_ref[...], preferred_element_type=F32)


def _proj_xbc_conv(hn, w_xbc, w_dt, conv_w, conv_b, tl=256):
    bsz, seq, _ = hn.shape
    ch = w_xbc.shape[1]
    tl = min(tl, seq)
    nblk = tl // HALO
    last_blk = seq // HALO - 1
    const = lambda b, i: (0, 0)
    return pl.pallas_call(
        _xbc_conv_kernel,
        out_shape=(jax.ShapeDtypeStruct((bsz, seq, ch), BF16), jax.ShapeDtypeStruct((bsz, seq, LANES), F32)),
        grid=(bsz, seq // tl),
        in_specs=[pl.BlockSpec((1, tl, D_MODEL), lambda b, i: (b, i, 0)),
                  pl.BlockSpec((1, HALO, D_MODEL), lambda b, i: (b, jnp.maximum(i * nblk - 1, 0), 0)),
                  pl.BlockSpec((1, HALO, D_MODEL), lambda b, i: (b, jnp.minimum((i + 1) * nblk, last_blk), 0)),
                  pl.BlockSpec((D_MODEL, ch), const),
                  pl.BlockSpec((D_MODEL, LANES), const),
                  pl.BlockSpec((SSD_CONV, ch), const),
                  pl.BlockSpec((1, ch), const)],
        out_specs=(pl.BlockSpec((1, tl, ch), lambda b, i: (b, i, 0)),
                   pl.BlockSpec((1, tl, LANES), lambda b, i: (b, i, 0))),
        scratch_shapes=[pltpu.VMEM((tl + 2 * HALO, CONV_COLS), F32), pltpu.VMEM((tl + 2 * HALO, CONV_COLS), F32)],
        compiler_params=_params(("parallel", "parallel")),
        name="proj_xbc_conv",
    )(hn, hn, hn, w_xbc, w_dt, conv_w, conv_b.reshape(1, ch))


def _ssd_kernel(xbc_ref, dt_ref, bias_ref, alog_ref, o_ref, state_ref, *, reverse):
    c = pl.program_id(1)

    @pl.when(c == 0)
    def _():
        state_ref[...] = jnp.zeros_like(state_ref)

    col0 = SSD_HEADS if reverse else 0
    row = lax.broadcasted_iota(jnp.int32, (CHUNK, CHUNK), 0)
    lane = lax.broadcasted_iota(jnp.int32, (CHUNK, CHUNK), 1)
    tri = (row <= lane) if reverse else (row >= lane)
    lo_half = lane < SSD_HEAD_DIM

    dt = _softplus(dt_ref[0] + bias_ref[...])
    adt = dt * (-jnp.exp(alog_ref[...]))
    cs = adt
    shift = 1
    while shift < CHUNK:
        if reverse:
            moved = pltpu.roll(cs, CHUNK - shift, axis=0)
            cs = cs + jnp.where(row < CHUNK - shift, moved, 0.0)
        else:
            moved = pltpu.roll(cs, shift, axis=0)
            cs = cs + jnp.where(row >= shift, moved, 0.0)
        shift *= 2
    cs_t = cs.T
    dt_t = dt.T
    end = 0 if reverse else CHUNK - 1
    cs_end = jnp.broadcast_to(cs_t[:, end:end + 1], (CHUNK, CHUNK))
    w_all = dt_t * jnp.exp(cs_end - cs_t)
    dec_all = jnp.exp(cs_end)

    hpg = SSD_HEADS // SSD_GROUPS
    for g in range(SSD_GROUPS):
        b_off = SSD_INNER + g * SSD_STATE
        c_off = SSD_INNER + SSD_GROUPS * SSD_STATE + g * SSD_STATE
        bg = xbc_ref[0, :, b_off:b_off + SSD_STATE]
        cg = xbc_ref[0, :, c_off:c_off + SSD_STATE]
        cb = lax.dot_general(cg, bg, (((1,), (1,)), ((), ())), preferred_element_type=F32)
        cg32 = cg.astype(F32)
        bg_t = bg.astype(F32).T
        for j in range(hpg // 2):
            pair = g * (hpg // 2) + j
            lhs_parts, lhs2_parts = [], []
            for k in range(2):
                col = col0 + 2 * pair + k
                lmat = jnp.broadcast_to(cs[:, col:col + 1], (CHUNK, CHUNK))
                decay = jnp.where(tri, jnp.exp(lmat - cs_t[col:col + 1, :]), 0.0)
                lhs_parts.append((cb * decay * dt_t[col:col + 1, :]).astype(BF16))
                lhs_parts.append((cg32 * jnp.exp(lmat)).astype(BF16))
                lhs2_parts.append((bg_t * w_all[col:col + 1, :]).astype(BF16))
            xs = xbc_ref[0, :, pair * LANES:(pair + 1) * LANES]
            zero = jnp.zeros_like(xs)
            x0 = jnp.where(lo_half, xs, zero)
            x1 = jnp.where(lo_half, zero, xs)
            st = state_ref[pair]
            st16 = st.astype(BF16)
            s0 = jnp.where(lo_half, st16, zero)
            s1 = jnp.where(lo_half, zero, st16)
            lhs = jnp.concatenate(lhs_parts, axis=1)
            rhs = jnp.concatenate([x0, s0, x1, s1], axis=0)
            o_ref[0, :, pair * LANES:(pair + 1) * LANES] = jnp.dot(
                lhs, rhs, preferred_element_type=F32).astype(o_ref.dtype)
            lhs2 = jnp.concatenate(lhs2_parts, axis=1)
            rhs2 = jnp.concatenate([x0, x1], axis=0)
            col_a = col0 + 2 * pair
            dec = jnp.where(lo_half, dec_all[col_a:col_a + 1, :], dec_all[col_a + 1:col_a + 2, :])
            state_ref[pair] = st * dec + jnp.dot(lhs2, rhs2, preferred_element_type=F32)


def _ssd_scan(xbc_act, dt_raw, dt_bias_row, alog_row, reverse):
    bsz, seq, _ = xbc_act.shape
    nc = seq // CHUNK
    if reverse:
        cmap = lambda b, c: (b, nc - 1 - c, 0)
    else:
        cmap = lambda b, c: (b, c, 0)
    return pl.pallas_call(
        functools.partial(_ssd_kernel, reverse=reverse),
        out_shape=jax.ShapeDtypeStruct((bsz, seq, SSD_INNER), BF16),
        grid=(bsz, nc),
        in_specs=[pl.BlockSpec((1, CHUNK, SSD_CONV_DIM), cmap),
                  pl.BlockSpec((1, CHUNK, LANES), cmap),
                  pl.BlockSpec((1, LANES), lambda b, c: (0, 0)),
                  pl.BlockSpec((1, LANES), lambda b, c: (0, 0))],
        out_specs=pl.BlockSpec((1, CHUNK, SSD_INNER), cmap),
        scratch_shapes=[pltpu.VMEM((SSD_HEADS // 2, SSD_STATE, LANES), F32)],
        compiler_params=_params(("parallel", "arbitrary")),
        name="ssd_bwd" if reverse else "ssd_fwd",
    )(xbc_act, dt_raw, dt_bias_row, alog_row)


def _tail_kernel(yf_ref, yb_ref, xs_ref, z_ref, gb_ref, pa_ref, x_ref, g1_ref, dskip_ref, gain_ref,
                 wb_ref, wo_ref, n2_ref, sh2_ref, sc2_ref, xo_ref, h2t_ref):
    y = yf_ref[...].astype(F32) + yb_ref[...].astype(F32) + dskip_ref[...] * xs_ref[...].astype(F32)
    y = y * z_ref[...].astype(F32)
    gw = SSD_INNER // SSD_GROUPS
    parts = []
    for g in range(SSD_GROUPS):
        yg = y[:, g * gw:(g + 1) * gw]
        parts.append(yg * lax.rsqrt(jnp.mean(yg * yg, axis=-1, keepdims=True) + EPS))
    yn = (jnp.concatenate(parts, axis=1) * gain_ref[...]).astype(BF16)
    pb = jnp.dot(yn, wb_ref[...], preferred_element_type=F32)
    merged = pa_ref[...].astype(F32) + gb_ref[...].astype(F32) * pb
    out = jnp.dot(merged.astype(BF16), wo_ref[...], preferred_element_type=F32)
    x = x_ref[...] + g1_ref[...] * out
    xo_ref[...] = x
    h = x * lax.rsqrt(jnp.mean(x * x, axis=-1, keepdims=True) + EPS) * n2_ref[...]
    h2t_ref[...] = (h * (1.0 + sc2_ref[...]) + sh2_ref[...]).T.astype(h2t_ref.dtype)


def _tail(yf, yb, xbc_act, z_act, gates, pa, x, mod, dskip_row, ssd_gain, wb, wo, norm2_gain, seq, tm=256):
    t = x.shape[0]
    tm = min(tm, seq)
    per_row = seq // tm
    row = lambda i: (i, 0)
    const = lambda i: (0, 0)
    modspec = lambda k: pl.BlockSpec((None, None, 1, D_MODEL), lambda i: (i // per_row, k, 0, 0))
    return pl.pallas_call(
        _tail_kernel,
        out_shape=(jax.ShapeDtypeStruct((t, D_MODEL), F32), jax.ShapeDtypeStruct((D_MODEL, t), BF16)),
        grid=(t // tm,),
        in_specs=[pl.BlockSpec((tm, SSD_INNER), row),
                  pl.BlockSpec((tm, SSD_INNER), row),
                  pl.BlockSpec((tm, SSD_INNER), row),
                  pl.BlockSpec((tm, SSD_INNER), row),
                  pl.BlockSpec((tm, D_MODEL), lambda i: (i, 1)),
                  pl.BlockSpec((tm, D_MODEL), row),
                  pl.BlockSpec((tm, D_MODEL), row),
                  modspec(2),
                  pl.BlockSpec((1, SSD_INNER), const),
                  pl.BlockSpec((1, SSD_INNER), const),
                  pl.BlockSpec((SSD_INNER, D_MODEL), const),
                  pl.BlockSpec((D_MODEL, D_MODEL), const),
                  pl.BlockSpec((1, D_MODEL), const),
                  modspec(3),
                  modspec(4)],
        out_specs=(pl.BlockSpec((tm, D_MODEL), row), pl.BlockSpec((D_MODEL, tm), lambda i: (0, i))),
        compiler_params=_params(("parallel",)),
        name="mix_tail",
    )(yf, yb, xbc_act, z_act, gates, pa, x, mod, dskip_row, ssd_gain.reshape(1, SSD_INNER), wb, wo,
      norm2_gain.reshape(1, D_MODEL), mod, mod)


CAND_PAIRS = [(a, b) for a in range(PEER_TOPK) for b in range(PEER_TOPK) if (a + 1) * (b + 1) <= PEER_TOPK]
RANK_OUTSIDE = float(2 * PEER_TOPK)
SUBLANES = 8


def _sorted_desc(cols):
    c = list(cols)
    n = len(c)
    k = 2
    while k <= n:
        j = k // 2
        while j >= 1:
            for i in range(n):
                l = i ^ j
                if l > i:
                    hi, lo = jnp.maximum(c[i], c[l]), jnp.minimum(c[i], c[l])
                    c[i], c[l] = (hi, lo) if (i & k) == 0 else (lo, hi)
            j //= 2
        k *= 2
    return c


def _top_values(s):
    c = _sorted_desc([s[v * SUBLANES:(v + 1) * SUBLANES, :] for v in range(PEER_NKEYS // SUBLANES)])
    tops = []
    for r in range(PEER_TOPK):
        m = jnp.max(c[0], axis=0, keepdims=True)
        tops.append(m)
        depth = PEER_TOPK - 1 - r
        if depth:
            hit = c[0] == m
            c = [jnp.where(hit, c[q + 1], c[q]) for q in range(depth)]
    return tops


def _count_at_least(x, thresholds):
    cnt = jnp.zeros(x.shape, F32)
    for n, th in enumerate(thresholds):
        cnt = jnp.where(x >= th, float(n + 1), cnt)
    return cnt


def _route_kernel(ht_ref, wq_ref, keys_ref, c0_ref, a_ref, r1_ref, b_ref, top_ref, s_ref):
    q_t = jnp.dot(wq_ref[...], ht_ref[...], preferred_element_type=F32)
    for hd in range(PEER_HEADS):
        for half in range(2):
            r0 = (hd * 2 + half) * PEER_HALF
            q = q_t[r0:r0 + PEER_HALF, :].astype(BF16)
            s = jnp.dot(keys_ref[half], q, preferred_element_type=F32)
            s_ref[half, hd] = s
            for r, m in enumerate(_top_values(s)):
                top_ref[half, r, hd:hd + 1, :] = m
    cands = [top_ref[0, a] + top_ref[1, b] for a, b in CAND_PAIRS]
    best = cands[0]
    work = list(cands)
    kth = best
    for r in range(PEER_TOPK + 1):
        prev = kth
        kth = work[0]
        for x in work[1:]:
            kth = jnp.maximum(kth, x)
        if r < PEER_TOPK:
            work = [jnp.where(x == kth, NEG_BIG, x) for x in work]
    tau = 0.5 * (prev + kth)
    z = jnp.zeros_like(best)
    for x in cands:
        z = z + jnp.where(x >= tau, jnp.exp(x - best), 0.0)
    inv_z = 1.0 / z
    for hd in range(PEER_HEADS):
        s0 = s_ref[0, hd]
        s1 = s_ref[1, hd]
        tau_h = tau[hd:hd + 1, :]
        tops1 = [top_ref[1, b, hd:hd + 1, :] for b in range(PEER_TOPK)]
        floor0 = top_ref[0, PEER_TOPK - 1, hd:hd + 1, :]
        c0_ref[hd] = _bf16_pair_words(_count_at_least(s0, [jnp.maximum(tau_h - t, floor0) for t in tops1]))
        reached = _count_at_least(s1, tops1[::-1])
        r1_ref[hd] = jnp.where(reached > 0.0, float(PEER_TOPK) - reached, RANK_OUTSIDE).astype(BF16)
        a_ref[hd] = _bf16_pair_words(jnp.exp(s0 - top_ref[0, 0, hd:hd + 1, :]))
        b_ref[hd] = (jnp.exp(s1 - tops1[0]) * inv_z[hd:hd + 1, :]).astype(BF16)


def _route(h2t, wq_t, keys, tt=256):
    t = h2t.shape[1]
    shp32 = jax.ShapeDtypeStruct((PEER_HEADS, PEER_NKEYS, t), jnp.uint32)
    shp16 = jax.ShapeDtypeStruct((PEER_HEADS, PEER_NKEYS, t), BF16)
    ospec = pl.BlockSpec((PEER_HEADS, PEER_NKEYS, tt), lambda i: (0, 0, i))
    return pl.pallas_call(
        _route_kernel,
        out_shape=(shp32, shp32, shp16, shp16),
        grid=(t // tt,),
        in_specs=[pl.BlockSpec((D_MODEL, tt), lambda i: (0, i)),
                  pl.BlockSpec((2 * PEER_HEADS * PEER_HALF, D_MODEL), lambda i: (0, 0)),
                  pl.BlockSpec((2, PEER_NKEYS, PEER_HALF), lambda i: (0, 0, 0))],
        out_specs=(ospec, ospec, ospec, ospec),
        scratch_shapes=[pltpu.VMEM((2, PEER_TOPK, PEER_HEADS, tt), F32),
                        pltpu.VMEM((2, PEER_HEADS, PEER_NKEYS, tt), F32)],
        compiler_params=_params(("parallel",)),
        name="peer_route",
    )(h2t, wq_t, keys)


E_BLK = 2048
I_BLK = E_BLK // PEER_NKEYS
E_SUB = 256
I_SUB = E_SUB // PEER_NKEYS
ROWS16 = 16
ACT_ROWS = 512


def _peer_kernel(ht_ref, u_ref, vt_ref, c0_ref, a_ref, r1_ref, b_ref, x_ref, g2_ref, gain_ref, sh_ref, sc_ref,
                 *refs, final_norm):
    n_out = 1 if final_norm else 2
    o_ref = refs[0]
    act0_ref, act1_ref, p_ref, acc_ref = refs[n_out:]
    e = pl.program_id(1)
    tt = ht_ref.shape[1]
    t_sub = act0_ref.shape[1]
    n_jp = PEER_NKEYS // ROWS16
    act_refs = (act0_ref, act1_ref)
    units = [(sb, th) for sb in range(E_BLK // E_SUB) for th in range(tt // t_sub)]

    @pl.when(e == 0)
    def _():
        acc_ref[...] = jnp.zeros_like(acc_ref)

    per_mm = act0_ref.shape[0] // E_SUB

    def activation_matmul(n):
        if n % per_mm:
            return
        sb, th = units[n]
        act_refs[(n // per_mm) % 2][...] = jnp.dot(u_ref[sb * E_SUB:(sb + per_mm) * E_SUB, :],
                                                   ht_ref[:, th * t_sub:(th + 1) * t_sub],
                                                   preferred_element_type=F32)

    def gated_activations(n):
        sb, th = units[n]
        act_ref = act_refs[(n // per_mm) % 2]
        a0 = (n % per_mm) * E_SUB
        ils = [sb * I_SUB + k for k in range(I_SUB)]
        for tc in range(t_sub // LANES):
            ls = slice(tc * LANES, (tc + 1) * LANES)
            ts = slice(th * t_sub + tc * LANES, th * t_sub + (tc + 1) * LANES)
            accs = [[None] * n_jp for _ in ils]
            for hd in range(PEER_HEADS):
                cnt = [_row_as_bf16_tile(c0_ref[hd, il:il + 1, ts]) for il in ils]
                wgt = [_row_as_bf16_tile(a_ref[hd, il:il + 1, ts]) for il in ils]
                for jp in range(n_jp):
                    js = slice(jp * ROWS16, (jp + 1) * ROWS16)
                    r1 = r1_ref[hd, js, ts]
                    bb = b_ref[hd, js, ts]
                    for k in range(I_SUB):
                        term = jnp.minimum(jnp.maximum(cnt[k] - r1, 0.0), wgt[k]) * bb
                        accs[k][jp] = term if accs[k][jp] is None else accs[k][jp] + term
            for k in range(I_SUB):
                for jp in range(n_jp):
                    r0 = k * PEER_NKEYS + jp * ROWS16
                    gel = _gelu_sigmoid_form(act_ref[a0 + r0:a0 + r0 + ROWS16, ls].astype(BF16))
                    p_ref[sb * E_SUB + r0:sb * E_SUB + r0 + ROWS16, ts] = accs[k][jp] * gel

    activation_matmul(0)
    for n in range(len(units)):
        if n % per_mm == 0 and n + per_mm < len(units):
            activation_matmul(n + per_mm)
        gated_activations(n)
    acc_ref[...] += jnp.dot(vt_ref[...], p_ref[...], preferred_element_type=F32)

    @pl.when(e == pl.num_programs(1) - 1)
    def _():
        x = x_ref[...] + g2_ref[...] * acc_ref[...].T
        y = x * lax.rsqrt(jnp.mean(x * x, axis=-1, keepdims=True) + EPS) * gain_ref[...]
        if final_norm:
            o_ref[...] = y
        else:
            o_ref[...] = x
            refs[1][...] = (y * (1.0 + sc_ref[...]) + sh_ref[...]).astype(BF16)


def _peer(h2t, u16, vt16, c0, a, r1, b, x, mod, gain, mod_next, seq, final_norm, tt=512):
    t = h2t.shape[1]
    tt = min(tt, seq)
    per_row = seq // tt
    ne = PEER_EXPERTS // E_BLK
    tok = lambda i, e: (i, 0)
    blk_i = pl.BlockSpec((PEER_HEADS, I_BLK, tt), lambda i, e: (0, e, i))
    full_j = pl.BlockSpec((PEER_HEADS, PEER_NKEYS, tt), lambda i, e: (0, 0, i))
    modspec = lambda tbl_idx: pl.BlockSpec((None, None, 1, D_MODEL), lambda i, e: (i // per_row, tbl_idx, 0, 0))
    x_shape = jax.ShapeDtypeStruct((t, D_MODEL), F32)
    x_spec = pl.BlockSpec((tt, D_MODEL), tok)
    return pl.pallas_call(
        functools.partial(_peer_kernel, final_norm=final_norm),
        out_shape=x_shape if final_norm else (x_shape, jax.ShapeDtypeStruct((t, D_MODEL), BF16)),
        grid=(t // tt, ne),
        in_specs=[pl.BlockSpec((D_MODEL, tt), lambda i, e: (0, i)),
                  pl.BlockSpec((E_BLK, D_MODEL), lambda i, e: (e, 0)),
                  pl.BlockSpec((D_MODEL, E_BLK), lambda i, e: (0, e)),
                  blk_i, blk_i, full_j, full_j,
                  x_spec,
                  modspec(5),
                  pl.BlockSpec((1, D_MODEL), lambda i, e: (0, 0)),
                  modspec(0),
                  modspec(1)],
        out_specs=x_spec if final_norm else (x_spec, x_spec),
        scratch_shapes=[pltpu.VMEM((ACT_ROWS, tt), F32), pltpu.VMEM((ACT_ROWS, tt), F32),
                        pltpu.VMEM((E_BLK, tt), BF16),
                        pltpu.VMEM((D_MODEL, tt), F32)],
        compiler_params=_params(("parallel", "arbitrary")),
        name="peer_dense",
    )(h2t, u16, vt16, c0, a, r1, b, x, mod, gain.reshape(1, D_MODEL), mod_next, mod_next)


def _prepare_weights(w_in, w_spatial, b_spatial, dt_bias, a_log, d_skip, w_proj_a, w_proj_b, w_out,
                     w_query, sub_keys, expert_u, expert_v):
    pad = LANES - 2 * SSD_HEADS
    w = {
        "w_uv": w_in[:, :, OFF_UV:OFF_Z].astype(BF16),
        "w_z": w_in[:, :, OFF_Z:OFF_XBC].astype(BF16),
        "w_xbc": w_in[:, :, OFF_XBC:OFF_DT].astype(BF16),
        "w_dt": jnp.pad(w_in[:, :, OFF_DT:OFF_GATE], ((0, 0), (0, 0), (0, pad))).astype(BF16),
        "w_gate": w_in[:, :, OFF_GATE:OFF_END].astype(BF16),
        "ws": w_spatial.astype(BF16),
        "bs": jnp.broadcast_to(b_spatial[..., None], b_spatial.shape + (CHUNK,)),
        "dt_bias": jnp.pad(dt_bias.reshape(DEPTH, 1, 2 * SSD_HEADS), ((0, 0), (0, 0), (0, pad))),
        "a_log": jnp.pad(a_log.reshape(DEPTH, 1, 2 * SSD_HEADS), ((0, 0), (0, 0), (0, pad))),
        "d_skip": jnp.repeat(d_skip, SSD_HEAD_DIM, axis=1).reshape(DEPTH, 1, SSD_INNER),
        "wa": w_proj_a.astype(BF16),
        "wb": w_proj_b.astype(BF16),
        "wo": w_out.astype(BF16),
        "wq_t": jnp.swapaxes(w_query, 1, 2).astype(BF16),
        "keys": sub_keys.astype(BF16),
        "u": expert_u.astype(BF16),
        "vt": jnp.swapaxes(expert_v, 1, 2).astype(BF16),
    }
    return w


def _trunk(x, c, w_mod, b_mod, norm1_gain, norm2_gain, sgu_gain, conv_w, conv_b, ssd_gain, final_gain, w):
    bsz, seq, _ = x.shape
    t = bsz * seq
    mod_all = _modulation(c, w_mod, b_mod)
    xf = x.reshape(t, D_MODEL)
    hn = _prenorm(x, norm1_gain[0], mod_all[0], 0, 1).reshape(t, D_MODEL)
    for l in range(DEPTH):
        mod = mod_all[l]
        uv = _matmul_act(hn, w["w_uv"][l], _gelu, BF16, 1024, "proj_uv")
        z_act = _matmul_act(hn, w["w_z"][l], _silu, BF16, 1024, "proj_z")
        gates = _matmul_act(hn, w["w_gate"][l], _sigmoid, BF16, 1024, "proj_gate")
        pa = _sgu(uv, gates, sgu_gain[l], w["ws"][l], w["bs"][l], w["wa"][l])
        xbc_act, dt3 = _proj_xbc_conv(hn.reshape(bsz, seq, D_MODEL), w["w_xbc"][l], w["w_dt"][l],
                                      conv_w[l], conv_b[l])
        yf = _ssd_scan(xbc_act, dt3, w["dt_bias"][l], w["a_log"][l], reverse=False)
        yb = _ssd_scan(xbc_act, dt3, w["dt_bias"][l], w["a_log"][l], reverse=True)
        xf, h2t = _tail(yf.reshape(t, SSD_INNER), yb.reshape(t, SSD_INNER),
                       xbc_act.reshape(t, SSD_CONV_DIM), z_act, gates, pa, xf, mod,
                       w["d_skip"][l], ssd_gain[l], w["wb"][l], w["wo"][l], norm2_gain[l], seq)
        c0, a, r1, b = _route(h2t, w["wq_t"][l], w["keys"][l])
        if l == DEPTH - 1:
            xf = _peer(h2t, w["u"][l], w["vt"][l], c0, a, r1, b, xf, mod, final_gain, mod, seq, final_norm=True)
        else:
            xf, hn = _peer(h2t, w["u"][l], w["vt"][l], c0, a, r1, b, xf, mod, norm1_gain[l + 1],
                           mod_all[l + 1], seq, final_norm=False)
    return xf.reshape(bsz, seq, D_MODEL)


def kernel(x_prompt, x_sample, c_prompt, c_sample, w_mod, b_mod, norm1_gain, norm2_gain, w_in, sgu_gain, w_spatial, b_spatial, conv_w, conv_b, dt_bias, a_log, d_skip, ssd_gain, w_proj_a, w_proj_b, w_out, w_query, sub_keys, expert_u, expert_v, final_gain):
    w = _prepare_weights(w_in, w_spatial, b_spatial, dt_bias, a_log, d_skip, w_proj_a, w_proj_b,
                         w_out, w_query, sub_keys, expert_u, expert_v)
    args = (w_mod, b_mod, norm1_gain, norm2_gain, sgu_gain, conv_w, conv_b, ssd_gain, final_gain, w)
    y_prompt = _trunk(x_prompt, c_prompt, *args)
    y_sample = _trunk(x_sample, c_sample, *args)
    return (y_prompt, y_sample)
```
